```python
import math
import jax, jax.numpy as jnp
from jax import lax
import numpy as np

D_MODEL = 1024
BATCH = 16
SEQ = 2048
DEPTH = 1
DEC_BATCH = 128
DEC_SEQ = 4
PAST_LEN = 16384
PAGE_SIZE = 128

MIX_WIDTH = D_MODEL
MLA_HEADS = 8
MLA_NOPE = 64
MLA_ROPE = 32
MLA_V = 64
Q_LORA = 384
KV_LORA = 256
ROPE_THETA = 10000.0
MLA_SCALE = (MLA_NOPE + MLA_ROPE) ** -0.5
DIFF_HEADS = 4
DIFF_QK = 64
DIFF_V = 2 * DIFF_QK
DIFF_SCALE = DIFF_QK ** -0.5
DIFF_Q_COLS = DIFF_HEADS * 2 * DIFF_QK
DIFF_V_COLS = DIFF_HEADS * DIFF_V
IN_COLS = Q_LORA + KV_LORA + MLA_ROPE + 2 * DIFF_Q_COLS + DIFF_V_COLS
N_EXPERTS = 32
TOP_K = 4
D_FF = D_MODEL
SWIGLU_LIMIT = 7.0
SWIGLU_ALPHA = 1.702
EPS = 1e-6
Q_BLOCK = 128

kernel_name = 'hymba_mla_diffattn_moe_step'


def rms_norm(x, g):
    xf = x.astype(jnp.float32)
    y = xf * lax.rsqrt(jnp.mean(xf * xf, axis=-1, keepdims=True) + EPS)
    return (y * g.astype(jnp.float32)).astype(x.dtype)


def rope(x, pos):
    half = MLA_ROPE // 2
    freq = ROPE_THETA ** (-2.0 * jnp.arange(half, dtype=jnp.float32) / MLA_ROPE)
    ang = pos[:, None] * freq[None, :]
    ang = ang.reshape((ang.shape[0],) + (1,) * (x.ndim - 3) + (half,))
    cos = jnp.cos(ang).astype(x.dtype)
    sin = jnp.sin(ang).astype(x.dtype)
    x1, x2 = x[..., :half], x[..., half:]
    return jnp.concatenate([x1 * cos - x2 * sin, x1 * sin + x2 * cos], axis=-1)


def alibi_slopes():
    return 2.0 ** (-8.0 * jnp.arange(1, DIFF_HEADS + 1, dtype=jnp.float32) / DIFF_HEADS)


def project_tokens(xn, pos, w_in, g_q_a, w_uq, g_kv_a, w_uk):
    b, s, _ = xn.shape
    h = xn @ w_in
    o1 = Q_LORA
    o2 = o1 + KV_LORA
    o3 = o2 + MLA_ROPE
    o4 = o3 + DIFF_Q_COLS
    o5 = o4 + DIFF_Q_COLS
    q_a, kv_a, k_r, d_q, d_k, d_v = jnp.split(h, [o1, o2, o3, o4, o5], axis=-1)
    q = (rms_norm(q_a, g_q_a) @ w_uq).reshape(b, s, MLA_HEADS, MLA_NOPE + MLA_ROPE)
    q_rope = rope(q[..., MLA_NOPE:], pos)
    q_lat = jnp.einsum('bshn,chn->bshc', q[..., :MLA_NOPE], w_uk)
    ckv = rms_norm(kv_a, g_kv_a)
    krope = rope(k_r, pos)
    dq = d_q.reshape(b, s, DIFF_HEADS, 2, DIFF_QK)
    dk = d_k.reshape(b, s, DIFF_HEADS, 2 * DIFF_QK)
    dv = d_v.reshape(b, s, DIFF_HEADS, DIFF_V)
    return q_lat, q_rope, dq, ckv, krope, dk, dv


def mix_attend(q_lat, q_rope, dq, ckv, krope, dk, dv, q_pos, k_pos, w_uv, g_subln, lam, lambda_init):
    b, sq = q_lat.shape[:2]
    mask = k_pos[None, :] <= q_pos[:, None]
    s_a = (jnp.einsum('bqhc,bkc->bhqk', q_lat, ckv)
           + jnp.einsum('bqhr,bkr->bhqk', q_rope, krope)).astype(jnp.float32) * MLA_SCALE
    p_a = jax.nn.softmax(jnp.where(mask, s_a, -jnp.inf), axis=-1).astype(ckv.dtype)
    o_lat = jnp.einsum('bhqk,bkc->bqhc', p_a, ckv)
    o_a = jnp.einsum('bqhc,chv->bqhv', o_lat, w_uv).reshape(b, sq, MLA_HEADS * MLA_V)
    dk2 = dk.reshape(dk.shape[:3] + (2, DIFF_QK))
    dist = (q_pos[:, None] - k_pos[None, :]).astype(jnp.float32)
    s_b = (jnp.einsum('bqhjd,bkhjd->bjhqk', dq, dk2).astype(jnp.float32) * DIFF_SCALE
           - alibi_slopes()[:, None, None] * dist)
    p_b = jax.nn.softmax(jnp.where(mask, s_b, -jnp.inf), axis=-1)
    a = (p_b[:, 0] - lam * p_b[:, 1]).astype(dv.dtype)
    o_b = jnp.einsum('bhqk,bkhv->bqhv', a, dv)
    o_b = (rms_norm(o_b, g_subln) * (1.0 - lambda_init)).reshape(b, sq, DIFF_HEADS * DIFF_V)
    return jnp.concatenate([o_a, o_b], axis=-1)


def moe_ffn(xn, w_router, b_router, w_gate_up, b_gate_up, w_down, b_down):
    def per_seq(xs):
        logits = (xs @ w_router + b_router).astype(jnp.float32)
        top_v, top_i = lax.top_k(logits, TOP_K)
        gates = jnp.einsum('nk,nke->ne', jax.nn.softmax(top_v, axis=-1),
                           jax.nn.one_hot(top_i, N_EXPERTS, dtype=jnp.float32)).astype(xs.dtype)
        hgu = jnp.einsum('nd,edf->nef', xs, w_gate_up) + b_gate_up
        gate = jnp.minimum(hgu[..., :D_FF], SWIGLU_LIMIT)
        up = jnp.clip(hgu[..., D_FF:], -SWIGLU_LIMIT, SWIGLU_LIMIT)
        act = (up + 1.0) * gate * jax.nn.sigmoid(SWIGLU_ALPHA * gate) * gates[..., None]
        return jnp.einsum('nef,efd->nd', act, w_down) + gates @ b_down
    return lax.map(per_seq, xn)


def setup_inputs(seed: int = 0) -> dict:
    key = jax.random.key(seed)
    ks = jax.random.split(key, 32)
    f32 = jnp.float32
    n_pages = PAST_LEN // PAGE_SIZE
    n_used = DEC_BATCH * n_pages
    n_pool = n_used + n_used // 4
    nrm = lambda k, shape, scale: jax.random.normal(k, shape, f32) * scale
    gain = lambda k, n: 1.0 + 0.02 * jax.random.normal(k, (DEPTH, n), f32)
    perm = jax.random.permutation(ks[6], n_pool)
    return {
        'x_prompt': nrm(ks[0], (BATCH, SEQ, D_MODEL), 1.0),
        'x_sample': nrm(ks[1], (DEC_BATCH, DEC_SEQ, D_MODEL), 1.0),
        'cache_ckv': nrm(ks[2], (DEPTH, n_pool, PAGE_SIZE, KV_LORA), 1.0),
        'cache_krope': nrm(ks[3], (DEPTH, n_pool, PAGE_SIZE, MLA_ROPE), 1.0),
        'cache_diff_k': nrm(ks[4], (DEPTH, n_pool, PAGE_SIZE, DIFF_HEADS, 2 * DIFF_QK), 1.0),
        'cache_diff_v': nrm(ks[5], (DEPTH, n_pool, PAGE_SIZE, DIFF_HEADS, DIFF_V), 1.0),
        'page_table': perm[:n_used].reshape(DEC_BATCH, n_pages).astype(jnp.int32),
        'g_attn': gain(ks[7], D_MODEL),
        'w_in': nrm(ks[8], (DEPTH, D_MODEL, IN_COLS), D_MODEL ** -0.5),
        'g_q_a': gain(ks[9], Q_LORA),
        'w_uq': nrm(ks[10], (DEPTH, Q_LORA, MLA_HEADS * (MLA_NOPE + MLA_ROPE)), Q_LORA ** -0.5),
        'g_kv_a': gain(ks[11], KV_LORA),
        'w_uk': nrm(ks[12], (DEPTH, KV_LORA, MLA_HEADS, MLA_NOPE), MLA_NOPE ** -0.5),
        'w_uv': nrm(ks[13], (DEPTH, KV_LORA, MLA_HEADS, MLA_V), KV_LORA ** -0.5),
        'lambda_q1': nrm(ks[14], (DEPTH, DIFF_QK), 0.1),
        'lambda_k1': nrm(ks[15], (DEPTH, DIFF_QK), 0.1),
        'lambda_q2': nrm(ks[16], (DEPTH, DIFF_QK), 0.1),
        'lambda_k2': nrm(ks[17], (DEPTH, DIFF_QK), 0.1),
        'g_subln': gain(ks[18], DIFF_V),
        'w_o': nrm(ks[19], (DEPTH, MIX_WIDTH, D_MODEL), MIX_WIDTH ** -0.5),
        'g_ffn': gain(ks[20], D_MODEL),
        'w_router': nrm(ks[21], (DEPTH, D_MODEL, N_EXPERTS), D_MODEL ** -0.5),
        'b_router': nrm(ks[22], (DEPTH, N_EXPERTS), 0.01),
        'w_gate_up': nrm(ks[23], (DEPTH, N_EXPERTS, D_MODEL, 2 * D_FF), D_MODEL ** -0.5),
        'b_gate_up': nrm(ks[24], (DEPTH, N_EXPERTS, 2 * D_FF), 0.02),
        'w_down': nrm(ks[25], (DEPTH, N_EXPERTS, D_FF, D_MODEL), D_FF ** -0.5),
        'b_down': nrm(ks[26], (DEPTH, N_EXPERTS, D_MODEL), 0.02),
        'g_final': 1.0 + 0.02 * jax.random.normal(ks[27], (D_MODEL,), f32),
    }


def reference(x_prompt, x_sample, cache_ckv, cache_krope, cache_diff_k, cache_diff_v, page_table,
              g_attn, w_in, g_q_a, w_uq, g_kv_a, w_uk, w_uv, lambda_q1, lambda_k1, lambda_q2, lambda_k2,
              g_subln, w_o, g_ffn, w_router, b_router, w_gate_up, b_gate_up, w_down, b_down, g_final):
    f32 = jnp.float32
    xp, xs = x_prompt, x_sample
    bp, sp, _ = xp.shape
    bs, ss, _ = xs.shape
    new_ckv_p, new_kr_p, new_dk_p, new_dv_p = [], [], [], []
    new_ckv_s, new_kr_s, new_dk_s, new_dv_s = [], [], [], []
    for l in range(DEPTH):
        lambda_init = 0.8 - 0.6 * math.exp(-0.3 * l)
        lam = (jnp.exp(jnp.sum(lambda_q1[l].astype(f32) * lambda_k1[l].astype(f32)))
               - jnp.exp(jnp.sum(lambda_q2[l].astype(f32) * lambda_k2[l].astype(f32))) + lambda_init)
        proj_w = (w_in[l], g_q_a[l], w_uq[l], g_kv_a[l], w_uk[l])
        mix_w = (w_uv[l], g_subln[l], lam, lambda_init)
        moe_w = (w_router[l], b_router[l], w_gate_up[l], b_gate_up[l], w_down[l], b_down[l])

        pos_p = jnp.arange(sp, dtype=f32)
        q_lat, q_rope, dq, ckv, krope, dk, dv = project_tokens(rms_norm(xp, g_attn[l]), pos_p, *proj_w)
        k_pos_p = jnp.arange(sp, dtype=jnp.int32)

        def prompt_block(i):
            st = i * Q_BLOCK
            sl = lambda a: lax.dynamic_slice_in_dim(a, st, Q_BLOCK, axis=1)
            q_pos = st + jnp.arange(Q_BLOCK, dtype=jnp.int32)
            return mix_attend(sl(q_lat), sl(q_rope), sl(dq), ckv, krope, dk, dv, q_pos, k_pos_p, *mix_w)

        o_p = lax.map(prompt_block, jnp.arange(sp // Q_BLOCK))
        o_p = jnp.moveaxis(o_p, 0, 1).reshape(bp, sp, MIX_WIDTH)
        hp = xp + o_p @ w_o[l]
        xp = hp + moe_ffn(rms_norm(hp, g_ffn[l]), *moe_w)
        new_ckv_p.append(ckv)
        new_kr_p.append(krope)
        new_dk_p.append(dk)
        new_dv_p.append(dv)

        pos_s = PAST_LEN + jnp.arange(ss, dtype=f32)
        s_q_lat, s_q_rope, s_dq, s_ckv, s_krope, s_dk, s_dv = project_tokens(rms_norm(xs, g_attn[l]), pos_s, *proj_w)
        q_pos_s = PAST_LEN + jnp.arange(ss, dtype=jnp.int32)
        k_pos_s = jnp.arange(PAST_LEN + ss, dtype=jnp.int32)

        def one_seq(args):
            pt, ql, qr, qd, ck_new, kr_new, dk_new, dv_new = args

            def gather(pool, new):
                past = pool[l, pt]
                past = past.reshape((1, pt.shape[0] * PAGE_SIZE) + pool.shape[3:])
                return jnp.concatenate([past, new[None].astype(past.dtype)], axis=1)

            out = mix_attend(ql[None], qr[None], qd[None],
                             gather(cache_ckv, ck_new), gather(cache_krope, kr_new),
                             gather(cache_diff_k, dk_new), gather(cache_diff_v, dv_new),
                             q_pos_s, k_pos_s, *mix_w)
            return out[0]

        o_s = lax.map(one_seq, (page_table, s_q_lat, s_q_rope, s_dq, s_ckv, s_krope, s_dk, s_dv))
        hs = xs + o_s @ w_o[l]
        xs = hs + moe_ffn(rms_norm(hs, g_ffn[l]), *moe_w)
        new_ckv_s.append(s_ckv)
        new_kr_s.append(s_krope)
        new_dk_s.append(s_dk)
        new_dv_s.append(s_dv)

    y_prompt = rms_norm(xp, g_final)
    y_sample = rms_norm(xs, g_final)
    return (y_prompt, y_sample,
            jnp.stack(new_ckv_p), jnp.stack(new_kr_p), jnp.stack(new_dk_p), jnp.stack(new_dv_p),
            jnp.stack(new_ckv_s), jnp.stack(new_kr_s), jnp.stack(new_dk_s), jnp.stack(new_dv_s))
```

```python
import functools
import math

import jax
import jax.numpy as jnp
from jax import lax
from jax.experimental import pallas as pl
from jax.experimental.pallas import tpu as pltpu

F32 = jnp.float32
BF16 = jnp.bfloat16

D_MODEL = 1024
MLA_HEADS = 8
MLA_NOPE = 64
MLA_ROPE = 32
MLA_V = 64
Q_LORA = 384
KV_LORA = 256
ROPE_THETA = 10000.0
MLA_SCALE = (MLA_NOPE + MLA_ROPE) ** -0.5
DIFF_HEADS = 4
DIFF_QK = 64
DIFF_V = 2 * DIFF_QK
DIFF_SCALE = DIFF_QK ** -0.5
DIFF_COLS = DIFF_HEADS * DIFF_V
N_EXPERTS = 32
TOP_K = 4
D_FF = D_MODEL
SWIGLU_LIMIT = 7.0
SWIGLU_ALPHA = 1.702
EPS = 1e-6

LANES = 128
ROPE_PAD = LANES
KVC = KV_LORA + ROPE_PAD
IN_PAD_COLS = Q_LORA + KV_LORA + ROPE_PAD + 3 * DIFF_COLS
NEG = -1e30
VMEM_LIMIT = 56 * 1024 * 1024

_NT = (((1,), (1,)), ((), ()))


def _rms(x, g):
    return x * lax.rsqrt(jnp.mean(x * x, axis=-1, keepdims=True) + EPS) * g


def _alibi_slope(h):
    return 0.25 ** (h + 1)


def _proj_kernel(x_ref, ga_ref, win_ref, gq_ref, wuq_ref, gkv_ref, wuk_ref, tq_ref, tk_ref,
                 ckv_ref, kr_ref, dk_ref, dv_ref, qm_ref, dq_ref, kvc_ref, dkb_ref, dvb_ref):
    tm = x_ref.shape[0]
    xn = _rms(x_ref[...], ga_ref[...])
    h = jnp.dot(xn.astype(BF16), win_ref[...], preferred_element_type=F32)
    o_kv = Q_LORA
    o_kr = o_kv + KV_LORA
    o_dq = o_kr + ROPE_PAD
    o_dk = o_dq + DIFF_COLS
    o_dv = o_dk + DIFF_COLS
    qn = _rms(h[:, :o_kv], gq_ref[...])
    q = jnp.dot(qn.astype(BF16), wuq_ref[...], preferred_element_type=F32)
    n_nope = MLA_HEADS * MLA_NOPE
    n_rope = MLA_HEADS * MLA_ROPE
    tq = tq_ref[...]
    q_rope = (q[:, n_nope:n_nope + n_rope] * tq[:, :n_rope]
              + q[:, n_nope + n_rope:] * tq[:, n_rope:])
    ckv = _rms(h[:, o_kv:o_kr], gkv_ref[...])
    kk = h[:, o_kr:o_dq] * tk_ref[...]
    krope = kk[:, :MLA_ROPE] + kk[:, MLA_ROPE:2 * MLA_ROPE]
    d_k = h[:, o_dk:o_dv]
    d_v = h[:, o_dv:]
    ckv_ref[...] = ckv
    kr_ref[...] = krope
    dk_ref[...] = d_k
    dv_ref[...] = d_v
    zpad = jnp.zeros((tm, ROPE_PAD - MLA_ROPE), F32)
    kvc_ref[...] = jnp.concatenate([ckv, krope, zpad], axis=-1).astype(BF16)
    dkb_ref[...] = d_k.astype(BF16)
    dvb_ref[...] = d_v.astype(BF16)
    dq_ref[...] = (h[:, o_dq:o_dk] * DIFF_SCALE).astype(BF16)
    for p in range(MLA_HEADS // 2):
        ql2 = jnp.dot(q[:, LANES * p:LANES * (p + 1)].astype(BF16), wuk_ref[p],
                      preferred_element_type=F32)
        for u in range(2):
            hh = 2 * p + u
            blk = jnp.concatenate(
                [ql2[:, KV_LORA * u:KV_LORA * (u + 1)],
                 q_rope[:, MLA_ROPE * hh:MLA_ROPE * (hh + 1)], zpad], axis=-1)
            qm_ref[hh] = (blk * MLA_SCALE).astype(BF16)


def _proj_call(x2, tq, tk, w, tm):
    n = x2.shape[0]
    ntab = tq.shape[0] // tm
    row = lambda i: (i, 0)
    c2 = lambda i: (0, 0)
    tab = lambda i: (i % ntab, 0)
    in_specs = [
        pl.BlockSpec((tm, D_MODEL), row),
        pl.BlockSpec((1, D_MODEL), c2),
        pl.BlockSpec((D_MODEL, IN_PAD_COLS), c2),
        pl.BlockSpec((1, Q_LORA), c2),
        pl.BlockSpec((Q_LORA, 2 * MLA_HEADS * MLA_NOPE), c2),
        pl.BlockSpec((1, KV_LORA), c2),
        pl.BlockSpec((MLA_HEADS // 2, LANES, 2 * KV_LORA), lambda i: (0, 0, 0)),
        pl.BlockSpec((tm, 2 * MLA_HEADS * MLA_ROPE), tab),
        pl.BlockSpec((tm, ROPE_PAD), tab),
    ]
    out_shape = [
        jax.ShapeDtypeStruct((n, KV_LORA), F32),
        jax.ShapeDtypeStruct((n, MLA_ROPE), F32),
        jax.ShapeDtypeStruct((n, DIFF_COLS), F32),
        jax.ShapeDtypeStruct((n, DIFF_COLS), F32),
        jax.ShapeDtypeStruct((MLA_HEADS, n, KVC), BF16),
        jax.ShapeDtypeStruct((n, DIFF_COLS), BF16),
        jax.ShapeDtypeStruct((n, KVC), BF16),
        jax.ShapeDtypeStruct((n, DIFF_COLS), BF16),
        jax.ShapeDtypeStruct((n, DIFF_COLS), BF16),
    ]
    out_specs = [
        pl.BlockSpec((tm, KV_LORA), row),
        pl.BlockSpec((tm, MLA_ROPE), row),
        pl.BlockSpec((tm, DIFF_COLS), row),
        pl.BlockSpec((tm, DIFF_COLS), row),
        pl.BlockSpec((MLA_HEADS, tm, KVC), lambda i: (0, i, 0)),
        pl.BlockSpec((tm, DIFF_COLS), row),
        pl.BlockSpec((tm, KVC), row),
        pl.BlockSpec((tm, DIFF_COLS), row),
        pl.BlockSpec((tm, DIFF_COLS), row),
    ]
    return pl.pallas_call(
        _proj_kernel,
        grid=(n // tm,),
        in_specs=in_specs,
        out_specs=out_specs,
        out_shape=out_shape,
        compiler_params=pltpu.CompilerParams(
            dimension_semantics=("arbitrary",), vmem_limit_bytes=VMEM_LIMIT),
        name="proj",
    )(x2, w["g_attn"], w["w_in"], w["g_q_a"], w["w_uq"], w["g_kv_a"], w["w_uk"], tq, tk)


def _online_update(s, m_ref, l_ref, acc_ref, pv_fn):
    m_prev = m_ref[...]
    m_next = jnp.maximum(m_prev, jnp.max(s, axis=-1, keepdims=True))
    alpha = jnp.exp(m_prev - m_next)
    p = jnp.exp(s - m_next[:, :1])
    l_ref[...] = alpha * l_ref[...] + jnp.sum(p, axis=-1, keepdims=True)
    m_ref[...] = m_next
    acc_ref[...] = acc_ref[...] * alpha[:, :1] + pv_fn(p.astype(BF16))


def _lambda_full(lq1_ref, lk1_ref, lq2_ref, lk2_ref, lambda_init):
    a = jnp.sum(lq1_ref[...] * lk1_ref[...], axis=-1, keepdims=True)
    b = jnp.sum(lq2_ref[...] * lk2_ref[...], axis=-1, keepdims=True)
    return jnp.exp(a) - jnp.exp(b) + lambda_init


def _prompt_attn_kernel(qm_ref, dq_ref, kvc_ref, dkb_ref, dvb_ref, wuv_ref, gs_ref,
                        lq1_ref, lk1_ref, lq2_ref, lk2_ref, o_ref,
                        m_a, l_a, acc_a, m_d, l_d, acc_d, *, bq, bk, lambda_init):
    i = pl.program_id(1)
    j = pl.program_id(2)
    last = (i * bq + bq - 1) // bk

    @pl.when(j == 0)
    def _init():
        m_a[...] = jnp.full(m_a.shape, NEG, F32)
        l_a[...] = jnp.zeros(l_a.shape, F32)
        acc_a[...] = jnp.zeros(acc_a.shape, F32)
        m_d[...] = jnp.full(m_d.shape, NEG, F32)
        l_d[...] = jnp.zeros(l_d.shape, F32)
        acc_d[...] = jnp.zeros(acc_d.shape, F32)

    @pl.when(j <= last)
    def _step():
        row = lax.broadcasted_iota(jnp.int32, (bq, bk), 0)
        col = lax.broadcasted_iota(jnp.int32, (bq, bk), 1)
        dist = (i * bq - j * bk) + row - col
        vis = dist >= 0
        distf = dist.astype(F32)

        q = qm_ref[...].reshape(MLA_HEADS * bq, KVC)
        kv = kvc_ref[...]
        s = lax.dot_general(q, kv, _NT, preferred_element_type=F32)
        s = jnp.where(vis[None], s.reshape(MLA_HEADS, bq, bk), NEG).reshape(MLA_HEADS * bq, bk)
        ckv = kv[:, :KV_LORA]
        _online_update(s, m_a, l_a, acc_a,
                       lambda p: jnp.dot(p, ckv, preferred_element_type=F32))

        lane = lax.broadcasted_iota(jnp.int32, (bq, DIFF_V), 1)
        for h in range(DIFF_HEADS):
            qh = dq_ref[:, DIFF_V * h:DIFF_V * (h + 1)]
            zero = jnp.zeros_like(qh)
            qbd = jnp.concatenate([jnp.where(lane < DIFF_QK, qh, zero),
                                   jnp.where(lane >= DIFF_QK, qh, zero)], axis=0)
            kh = dkb_ref[:, DIFF_V * h:DIFF_V * (h + 1)]
            sd = lax.dot_general(qbd, kh, _NT, preferred_element_type=F32)
            bias = jnp.where(vis, -_alibi_slope(h) * distf, NEG)
            sd = (sd.reshape(2, bq, bk) + bias[None]).reshape(2 * bq, bk)
            vh = dvb_ref[:, DIFF_V * h:DIFF_V * (h + 1)]
            _online_update(sd, m_d.at[h], l_d.at[h], acc_d.at[h],
                           lambda p, vh=vh: jnp.dot(p, vh, preferred_element_type=F32))

    @pl.when(j == last)
    def _finalize():
        o_lat = (acc_a[...] / l_a[:, :1]).astype(BF16).reshape(MLA_HEADS, bq, KV_LORA)
        oa = jnp.dot(o_lat[0], wuv_ref[0], preferred_element_type=F32)
        for h in range(1, MLA_HEADS):
            oa = oa + jnp.dot(o_lat[h], wuv_ref[h], preferred_element_type=F32)
        lam = _lambda_full(lq1_ref, lk1_ref, lq2_ref, lk2_ref, lambda_init)
        outs = [oa]
        for h in range(DIFF_HEADS):
            od = acc_d[h] / l_d[h][:, :1]
            ob = od[:bq] - lam * od[bq:]
            outs.append(_rms(ob, gs_ref[...]) * (1.0 - lambda_init))
        o_ref[...] = jnp.concatenate(outs, axis=-1).astype(o_ref.dtype)


def _prompt_attn_call(qm, dq, kvc, dkb, dvb, w, batch, seq, lambda_init, bq=128, bk=256):
    n = batch * seq
    nq = seq // bq
    nk = seq // bk
    kv_idx = lambda b, i, j: (b * nk + jnp.minimum(j, (i * bq + bq - 1) // bk), 0)
    q_idx = lambda b, i, j: (b * nq + i, 0)
    c2 = lambda b, i, j: (0, 0)
    in_specs = [
        pl.BlockSpec((MLA_HEADS, bq, KVC), lambda b, i, j: (0, b * nq + i, 0)),
        pl.BlockSpec((bq, DIFF_COLS), q_idx),
        pl.BlockSpec((bk, KVC), kv_idx),
        pl.BlockSpec((bk, DIFF_COLS), kv_idx),
        pl.BlockSpec((bk, DIFF_COLS), kv_idx),
        pl.BlockSpec((MLA_HEADS, KV_LORA, MLA_HEADS * MLA_V), lambda b, i, j: (0, 0, 0)),
        pl.BlockSpec((1, DIFF_V), c2),
        pl.BlockSpec((1, DIFF_QK), c2),
        pl.BlockSpec((1, DIFF_QK), c2),
        pl.BlockSpec((1, DIFF_QK), c2),
        pl.BlockSpec((1, DIFF_QK), c2),
    ]
    scratch = [
        pltpu.VMEM((MLA_HEADS * bq, LANES), F32),
        pltpu.VMEM((MLA_HEADS * bq, LANES), F32),
        pltpu.VMEM((MLA_HEADS * bq, KV_LORA), F32),
        pltpu.VMEM((DIFF_HEADS, 2 * bq, LANES), F32),
        pltpu.VMEM((DIFF_HEADS, 2 * bq, LANES), F32),
        pltpu.VMEM((DIFF_HEADS, 2 * bq, DIFF_V), F32),
    ]
    return pl.pallas_call(
        functools.partial(_prompt_attn_kernel, bq=bq, bk=bk, lambda_init=lambda_init),
        grid=(batch, nq, nk),
        in_specs=in_specs,
        out_specs=pl.BlockSpec((bq, D_MODEL), q_idx),
        out_shape=jax.ShapeDtypeStruct((n, D_MODEL), BF16),
        scratch_shapes=scratch,
        compiler_params=pltpu.CompilerParams(
            dimension_semantics=("arbitrary", "arbitrary", "arbitrary"),
            vmem_limit_bytes=VMEM_LIMIT),
        name="prompt_attn",
    )(qm, dq, kvc, dkb, dvb, w["w_uv_pad"], w["g_subln"],
      w["lambda_q1"], w["lambda_k1"], w["lambda_q2"], w["lambda_k2"])


def _sample_attn_kernel(pt_ref, q_ref, dqt_ref, kvn_ref, dkn_ref, dvn_ref, wuv_ref, gs_ref,
                        lq1_ref, lk1_ref, lq2_ref, lk2_ref, *rest,
                        pps, page, past_len, dec_seq, lambda_init):
    del pt_ref
    ckv_refs = rest[0:pps]
    kr_refs = rest[pps:2 * pps]
    dk_refs = rest[2 * pps:3 * pps]
    dv_refs = rest[3 * pps:4 * pps]
    o_ref = rest[4 * pps]
    m_a, l_a, acc_a, m_d, l_d, acc_d = rest[4 * pps + 1:]
    step = pl.program_id(1)
    nsteps = pl.num_programs(1)
    rows = dec_seq * MLA_HEADS
    hrows = 2 * dec_seq
    assert rows == DIFF_HEADS * hrows

    q = q_ref[...]
    r1 = lax.broadcasted_iota(jnp.int32, (rows, 1), 0)
    t_a = (r1 // MLA_HEADS).astype(F32)
    t_d = (r1 % dec_seq).astype(F32)
    h_d = r1 // hrows
    slope = jnp.where(h_d == 0, _alibi_slope(0),
                      jnp.where(h_d == 1, _alibi_slope(1),
                                jnp.where(h_d == 2, _alibi_slope(2), _alibi_slope(3))))
    r_q = lax.broadcasted_iota(jnp.int32, (rows, DIFF_V), 0)
    c_q = lax.broadcasted_iota(jnp.int32, (rows, DIFF_V), 1)
    own = (c_q // DIFF_QK) == ((r_q // dec_seq) % 2)
    qbd = jnp.where(own, dqt_ref[...], 0.0).astype(BF16)

    def head_rows(a, h):
        return a[hrows * h:hrows * (h + 1)]

    @pl.when(step == 0)
    def _new_tokens():
        m_a[...] = jnp.full(m_a.shape, NEG, F32)
        l_a[...] = jnp.zeros(l_a.shape, F32)
        acc_a[...] = jnp.zeros(acc_a.shape, F32)
        m_d[...] = jnp.full(m_d.shape, NEG, F32)
        l_d[...] = jnp.zeros(l_d.shape, F32)
        acc_d[...] = jnp.zeros(acc_d.shape, F32)
        npad = kvn_ref.shape[0]
        c = lax.broadcasted_iota(jnp.int32, (rows, npad), 1).astype(F32)
        kvn = kvn_ref[...]
        s = lax.dot_general(q, kvn, _NT, preferred_element_type=F32)
        s = jnp.where(c <= t_a, s, NEG)
        _online_update(s, m_a, l_a, acc_a,
                       lambda p: jnp.dot(p, kvn[:, :KV_LORA], preferred_element_type=F32))
        sd = jnp.concatenate(
            [lax.dot_general(head_rows(qbd, h), dkn_ref[:, DIFF_V * h:DIFF_V * (h + 1)], _NT,
                             preferred_element_type=F32) for h in range(DIFF_HEADS)], axis=0)
        sd = sd + jnp.where(c <= t_d, -slope * (t_d - c), NEG)
        _online_update(sd, m_d, l_d, acc_d, lambda p: jnp.concatenate(
            [jnp.dot(head_rows(p, h), dvn_ref[:, DIFF_V * h:DIFF_V * (h + 1)],
                     preferred_element_type=F32) for h in range(DIFF_HEADS)], axis=0))

    def head_page(ref, h):
        return ref[pl.ds(h, page, stride=DIFF_HEADS), :].astype(BF16)

    cks, s_parts = [], []
    sd_parts = [[] for _ in range(DIFF_HEADS)]
    zk = jnp.zeros((page, ROPE_PAD - MLA_ROPE), F32)
    for p in range(pps):
        ck = ckv_refs[p][...].astype(BF16)
        krp = jnp.concatenate([kr_refs[p][...], zk], axis=-1).astype(BF16)
        cks.append(ck)
        s_parts.append(lax.dot_general(q[:, :KV_LORA], ck, _NT, preferred_element_type=F32)
                       + lax.dot_general(q[:, KV_LORA:], krp, _NT, preferred_element_type=F32))
        for h in range(DIFF_HEADS):
            sd_parts[h].append(lax.dot_general(head_rows(qbd, h), head_page(dk_refs[p], h), _NT,
                                               preferred_element_type=F32))
    s = jnp.concatenate(s_parts, axis=-1)
    sd = jnp.concatenate([jnp.concatenate(parts, axis=-1) for parts in sd_parts], axis=0)
    kpos = (step * (pps * page)
            + lax.broadcasted_iota(jnp.int32, (rows, pps * page), 1)).astype(F32)
    sd = sd - slope * ((past_len + t_d) - kpos)

    def pv_a(p):
        acc = jnp.dot(p[:, :page], cks[0], preferred_element_type=F32)
        for u in range(1, pps):
            acc = acc + jnp.dot(p[:, page * u:page * (u + 1)], cks[u],
                                preferred_element_type=F32)
        return acc

    def pv_d(p):
        outs = []
        for h in range(DIFF_HEADS):
            ph = head_rows(p, h)
            acc = jnp.dot(ph[:, :page], head_page(dv_refs[0], h), preferred_element_type=F32)
            for u in range(1, pps):
                acc = acc + jnp.dot(ph[:, page * u:page * (u + 1)], head_page(dv_refs[u], h),
                                    preferred_element_type=F32)
            outs.append(acc)
        return jnp.concatenate(outs, axis=0)

    _online_update(s, m_a, l_a, acc_a, pv_a)
    _online_update(sd, m_d, l_d, acc_d, pv_d)

    @pl.when(step == nsteps - 1)
    def _finalize():
        n_v = MLA_HEADS * MLA_V
        o_lat = (acc_a[...] / l_a[:, :1]).astype(BF16)
        full = jnp.dot(o_lat, wuv_ref[...], preferred_element_type=F32)
        r_a = lax.broadcasted_iota(jnp.int32, (rows, n_v), 0)
        c_a = lax.broadcasted_iota(jnp.int32, (rows, n_v), 1)
        fa = jnp.where((c_a // MLA_V) == (r_a % MLA_HEADS), full, 0.0)
        oa = jnp.concatenate(
            [jnp.sum(fa[MLA_HEADS * t:MLA_HEADS * (t + 1)], axis=0, keepdims=True)
             for t in range(dec_seq)], axis=0)
        lam = _lambda_full(lq1_ref, lk1_ref, lq2_ref, lk2_ref, lambda_init)
        od = acc_d[...] / l_d[:, :1]
        segs = [oa]
        for h in range(DIFF_HEADS):
            oh = head_rows(od, h)
            ob = oh[:dec_seq] - lam * oh[dec_seq:]
            segs.append(_rms(ob, gs_ref[...]) * (1.0 - lambda_init))
        o_ref[...] = jnp.concatenate(segs, axis=-1)


def _sample_attn_call(page_table, q_s, dq_t, kv_new, dk_new, dv_new, caches, w, lambda_init,
                      pps=8):
    cache_ckv, cache_kr, cache_dk, cache_dv = caches
    nb, n_pages = page_table.shape
    page = cache_ckv.shape[1]
    dec_seq = q_s.shape[1] // MLA_HEADS
    rows = q_s.shape[1]
    npad = kv_new.shape[1]
    assert n_pages % pps == 0
    nsteps = n_pages // pps
    seq3 = lambda b, j, pt: (b, 0, 0)
    c2 = lambda b, j, pt: (0, 0)
    in_specs = [
        pl.BlockSpec((None, rows, KVC), seq3),
        pl.BlockSpec((None, rows, DIFF_V), seq3),
        pl.BlockSpec((None, npad, KVC), seq3),
        pl.BlockSpec((None, npad, DIFF_COLS), seq3),
        pl.BlockSpec((None, npad, DIFF_COLS), seq3),
        pl.BlockSpec((KV_LORA, MLA_HEADS * MLA_V), c2),
        pl.BlockSpec((1, DIFF_V), c2),
        pl.BlockSpec((1, DIFF_QK), c2),
        pl.BlockSpec((1, DIFF_QK), c2),
        pl.BlockSpec((1, DIFF_QK), c2),
        pl.BlockSpec((1, DIFF_QK), c2),
    ]
    args = [q_s, dq_t, kv_new, dk_new, dv_new, w["w_uv_all"], w["g_subln"],
            w["lambda_q1"], w["lambda_k1"], w["lambda_q2"], w["lambda_k2"]]
    for cache in (cache_ckv, cache_kr, cache_dk, cache_dv):
        for p in range(pps):
            in_specs.append(pl.BlockSpec(
                (None,) + cache.shape[1:],
                lambda b, j, pt, p=p: (pt[b, j * pps + p], 0, 0)))
            args.append(cache)
    scratch = [
        pltpu.VMEM((rows, LANES), F32),
        pltpu.VMEM((rows, LANES), F32),
        pltpu.VMEM((rows, KV_LORA), F32),
        pltpu.VMEM((rows, LANES), F32),
        pltpu.VMEM((rows, LANES), F32),
        pltpu.VMEM((rows, DIFF_V), F32),
    ]
    grid_spec = pltpu.PrefetchScalarGridSpec(
        num_scalar_prefetch=1,
        grid=(nb, nsteps),
        in_specs=in_specs,
        out_specs=pl.BlockSpec((None, dec_seq, D_MODEL), seq3),
        scratch_shapes=scratch,
    )
    return pl.pallas_call(
        functools.partial(_sample_attn_kernel, pps=pps, page=page,
                          past_len=float(n_pages * page), dec_seq=dec_seq,
                          lambda_init=lambda_init),
        grid_spec=grid_spec,
        out_shape=jax.ShapeDtypeStruct((nb, dec_seq, D_MODEL), F32),
        compiler_params=pltpu.CompilerParams(
            dimension_semantics=("arbitrary", "arbitrary"), vmem_limit_bytes=VMEM_LIMIT),
        name="sample_attn",
    )(page_table, *args)


def _outproj_router_kernel(o_ref, x_ref, wo_ref, gf_ref, wr_ref, br_ref,
                           hp_ref, xn_ref, gates_ref):
    tm = x_ref.shape[0]
    hp = x_ref[...] + jnp.dot(o_ref[...].astype(BF16), wo_ref[...], preferred_element_type=F32)
    hp_ref[...] = hp
    xn = _rms(hp, gf_ref[...])
    xn_ref[...] = xn.astype(BF16)
    logits = jnp.dot(xn, wr_ref[...], preferred_element_type=F32,
                     precision=lax.Precision.HIGHEST) + br_ref[...]
    lane = lax.broadcasted_iota(jnp.int32, (tm, N_EXPERTS), 1).astype(F32)
    work = logits
    sel = jnp.zeros((tm, N_EXPERTS), F32)
    m1 = None
    for k in range(TOP_K):
        mx = jnp.max(work, axis=-1, keepdims=True)
        if k == 0:
            m1 = mx
        idx = jnp.min(jnp.where(work == mx, lane, float(N_EXPERTS)), axis=-1, keepdims=True)
        hit = lane == idx
        sel = jnp.where(hit, 1.0, sel)
        work = jnp.where(hit, -jnp.inf, work)
    e = jnp.where(sel > 0.0, jnp.exp(logits - m1), 0.0)
    gates_ref[...] = e / jnp.sum(e, axis=-1, keepdims=True)


def _outproj_router_call(o2, x2, w, tm):
    n = x2.shape[0]
    row = lambda i: (i, 0)
    c2 = lambda i: (0, 0)
    return pl.pallas_call(
        _outproj_router_kernel,
        grid=(n // tm,),
        in_specs=[
            pl.BlockSpec((tm, D_MODEL), row),
            pl.BlockSpec((tm, D_MODEL), row),
            pl.BlockSpec((D_MODEL, D_MODEL), c2),
            pl.BlockSpec((1, D_MODEL), c2),
            pl.BlockSpec((D_MODEL, N_EXPERTS), c2),
            pl.BlockSpec((1, N_EXPERTS), c2),
        ],
        out_specs=[
            pl.BlockSpec((tm, D_MODEL), row),
            pl.BlockSpec((tm, D_MODEL), row),
            pl.BlockSpec((tm, N_EXPERTS), row),
        ],
        out_shape=[
            jax.ShapeDtypeStruct((n, D_MODEL), F32),
            jax.ShapeDtypeStruct((n, D_MODEL), BF16),
            jax.ShapeDtypeStruct((n, N_EXPERTS), F32),
        ],
        compiler_params=pltpu.CompilerParams(
            dimension_semantics=("arbitrary",), vmem_limit_bytes=VMEM_LIMIT),
        name="outproj_router",
    )(o2, x2, w["w_o"], w["g_ffn"], w["w_router"], w["b_router"])


def _moe_kernel(xn_ref, gates_ref, hp_ref, wgu_ref, bgu_ref, wd_ref, bd_ref, gfin_ref,
                y_ref, acc_ref, *, final_norm):
    e = pl.program_id(1)
    tm = xn_ref.shape[0]
    gates = gates_ref[...]

    @pl.when(e == 0)
    def _init():
        acc_ref[...] = hp_ref[...] + jnp.dot(gates, bd_ref[...], preferred_element_type=F32,
                                             precision=lax.Precision.HIGHEST)

    lane = lax.broadcasted_iota(jnp.int32, (tm, N_EXPERTS), 1)
    g_e = jnp.sum(jnp.where(lane == e, gates, 0.0), axis=-1, keepdims=True)
    hgu = jnp.dot(xn_ref[...], wgu_ref[...], preferred_element_type=F32) + bgu_ref[...]
    gate = jnp.minimum(hgu[:, :D_FF], SWIGLU_LIMIT)
    up = jnp.clip(hgu[:, D_FF:], -SWIGLU_LIMIT, SWIGLU_LIMIT)
    act = (up + 1.0) * gate * (1.0 / (1.0 + jnp.exp(-SWIGLU_ALPHA * gate))) * g_e
    acc_ref[...] += jnp.dot(act.astype(BF16), wd_ref[...], preferred_element_type=F32)

    @pl.when(e == pl.num_programs(1) - 1)
    def _out():
        y = acc_ref[...]
        y_ref[...] = _rms(y, gfin_ref[...]) if final_norm else y


def _moe_call(xn2, gates, hp, w, g_final, final_norm, tm):
    n = xn2.shape[0]
    row = lambda i, e: (i, 0)
    c2 = lambda i, e: (0, 0)
    return pl.pallas_call(
        functools.partial(_moe_kernel, final_norm=final_norm),
        grid=(n // tm, N_EXPERTS),
        in_specs=[
            pl.BlockSpec((tm, D_MODEL), row),
            pl.BlockSpec((tm, N_EXPERTS), row),
            pl.BlockSpec((tm, D_MODEL), row),
            pl.BlockSpec((None, D_MODEL, 2 * D_FF), lambda i, e: (e, 0, 0)),
            pl.BlockSpec((None, 1, 2 * D_FF), lambda i, e: (e, 0, 0)),
            pl.BlockSpec((None, D_FF, D_MODEL), lambda i, e: (e, 0, 0)),
            pl.BlockSpec((N_EXPERTS, D_MODEL), c2),
            pl.BlockSpec((1, D_MODEL), c2),
        ],
        out_specs=pl.BlockSpec((tm, D_MODEL), row),
        out_shape=jax.ShapeDtypeStruct((n, D_MODEL), F32),
        scratch_shapes=[pltpu.VMEM((tm, D_MODEL), F32)],
        compiler_params=pltpu.CompilerParams(
            dimension_semantics=("arbitrary", "arbitrary"), vmem_limit_bytes=VMEM_LIMIT),
        name="moe",
    )(xn2, gates, hp, w["w_gate_up"], w["b_gate_up"], w["w_down"], w["b_down"], g_final)


def _rotate_half_cols(wr):
    half = MLA_ROPE // 2
    return jnp.concatenate([-wr[..., half:], wr[..., :half]], axis=-1)


def _prep_layer(l, w_in, g_attn, g_q_a, w_uq, g_kv_a, w_uk, w_uv, lq1, lk1, lq2, lk2, g_subln,
                w_o, g_ffn, w_router, b_router, w_gate_up, b_gate_up, w_down, b_down):
    o_kr = Q_LORA + KV_LORA
    wi = w_in[l]
    k_r = wi[:, o_kr:o_kr + MLA_ROPE]
    w_in_p = jnp.concatenate(
        [wi[:, :o_kr], k_r, _rotate_half_cols(k_r),
         jnp.zeros((D_MODEL, ROPE_PAD - 2 * MLA_ROPE), F32), wi[:, o_kr + MLA_ROPE:]], axis=1)
    wq = w_uq[l].reshape(Q_LORA, MLA_HEADS, MLA_NOPE + MLA_ROPE)
    rope = wq[:, :, MLA_NOPE:]
    w_uq_p = jnp.concatenate(
        [wq[:, :, :MLA_NOPE].reshape(Q_LORA, -1), rope.reshape(Q_LORA, -1),
         _rotate_half_cols(rope).reshape(Q_LORA, -1)], axis=1)
    wk = jnp.transpose(w_uk[l], (1, 2, 0))
    z = jnp.zeros((MLA_NOPE, KV_LORA), F32)
    w_uk_bd = jnp.stack([
        jnp.concatenate([jnp.concatenate([wk[2 * p], z], axis=1),
                         jnp.concatenate([z, wk[2 * p + 1]], axis=1)], axis=0)
        for p in range(MLA_HEADS // 2)])
    wv = jnp.transpose(w_uv[l], (1, 0, 2))
    eye = jnp.eye(MLA_HEADS, dtype=F32)
    w_uv_pad = (wv[:, :, None, :] * eye[:, None, :, None]).reshape(
        MLA_HEADS, KV_LORA, MLA_HEADS * MLA_V)
    w_uv_all = w_uv[l].reshape(KV_LORA, MLA_HEADS * MLA_V)
    return dict(
        g_attn=g_attn[l][None], w_in=w_in_p.astype(BF16), g_q_a=g_q_a[l][None],
        w_uq=w_uq_p.astype(BF16), g_kv_a=g_kv_a[l][None], w_uk=w_uk_bd.astype(BF16),
        w_uv_pad=w_uv_pad.astype(BF16), w_uv_all=w_uv_all.astype(BF16),
        lambda_q1=lq1[l][None], lambda_k1=lk1[l][None], lambda_q2=lq2[l][None],
        lambda_k2=lk2[l][None], g_subln=g_subln[l][None],
        w_o=w_o[l].astype(BF16), g_ffn=g_ffn[l][None], w_router=w_router[l],
        b_router=b_router[l][None], w_gate_up=w_gate_up[l].astype(BF16),
        b_gate_up=b_gate_up[l][:, None, :], w_down=w_down[l].astype(BF16), b_down=b_down[l])


def _rope_tables(pos):
    half = MLA_ROPE // 2
    freq = ROPE_THETA ** (-2.0 * jnp.arange(half, dtype=F32) / MLA_ROPE)
    ang = pos[:, None] * freq[None, :]
    cos = jnp.cos(ang)
    sin = jnp.sin(ang)
    cos2 = jnp.concatenate([cos, cos], axis=1)
    sin2 = jnp.concatenate([sin, sin], axis=1)
    tq = jnp.concatenate([jnp.tile(cos2, (1, MLA_HEADS)), jnp.tile(sin2, (1, MLA_HEADS))], axis=1)
    tk = jnp.concatenate(
        [cos2, sin2, jnp.zeros((pos.shape[0], ROPE_PAD - 2 * MLA_ROPE), F32)], axis=1)
    return tq, tk


def _token_tile(n, target):
    t = min(n, target)
    assert n % t == 0
    return t


def kernel(x_prompt, x_sample, cache_ckv, cache_krope, cache_diff_k, cache_diff_v, page_table,
           g_attn, w_in, g_q_a, w_uq, g_kv_a, w_uk, w_uv, lambda_q1, lambda_k1, lambda_q2,
           lambda_k2, g_subln, w_o, g_ffn, w_router, b_router, w_gate_up, b_gate_up, w_down,
           b_down, g_final):
    bp, sp, _ = x_prompt.shape
    bs, ss, _ = x_sample.shape
    depth = w_in.shape[0]
    n_pool, page = cache_ckv.shape[1], cache_ckv.shape[2]
    n_pages = page_table.shape[1]
    past_len = n_pages * page
    n_p = bp * sp
    n_s = bs * ss
    new_pad = LANES

    tq_p, tk_p = _rope_tables(jnp.arange(sp, dtype=F32))
    tq_s, tk_s = _rope_tables(past_len + jnp.arange(ss, dtype=F32))
    tq_s = jnp.tile(tq_s, (bs, 1))
    tk_s = jnp.tile(tk_s, (bs, 1))
    g_fin = g_final[None]

    xp = x_prompt.reshape(n_p, D_MODEL)
    xs = x_sample.reshape(n_s, D_MODEL)
    outs = [[] for _ in range(8)]
    for l in range(depth):
        lambda_init = 0.8 - 0.6 * math.exp(-0.3 * l)
        last = l == depth - 1
        w = _prep_layer(l, w_in, g_attn, g_q_a, w_uq, g_kv_a, w_uk, w_uv, lambda_q1, lambda_k1,
                        lambda_q2, lambda_k2, g_subln, w_o, g_ffn, w_router, b_router,
                        w_gate_up, b_gate_up, w_down, b_down)

        tm_p = _token_tile(sp, 512)
        ckv, kr, dk, dv, qm, dq, kvc, dkb, dvb = _proj_call(xp, tq_p, tk_p, w, tm_p)
        o_p = _prompt_attn_call(qm, dq, kvc, dkb, dvb, w, bp, sp, lambda_init)
        hp, xn2, gates = _outproj_router_call(o_p, xp, w, tm_p)
        xp = _moe_call(xn2, gates, hp, w, g_fin, last, tm_p)
        outs[0].append(ckv.reshape(bp, sp, KV_LORA))
        outs[1].append(kr.reshape(bp, sp, MLA_ROPE))
        outs[2].append(dk.reshape(bp, sp, DIFF_HEADS, 2 * DIFF_QK))
        outs[3].append(dv.reshape(bp, sp, DIFF_HEADS, DIFF_V))

        tm_s = _token_tile(n_s, 512)
        ckv_s, kr_s, dk_s, dv_s, qm_s, dq_s, kvc_s, dkb_s, dvb_s = _proj_call(
            xs, tq_s, tk_s, w, tm_s)
        q_s = qm_s.reshape(MLA_HEADS, bs, ss, KVC).transpose(1, 2, 0, 3).reshape(
            bs, ss * MLA_HEADS, KVC)
        dq_t = jnp.broadcast_to(
            dq_s.astype(F32).reshape(bs, ss, DIFF_HEADS, 1, DIFF_V).transpose(0, 2, 3, 1, 4),
            (bs, DIFF_HEADS, 2, ss, DIFF_V)).reshape(bs, DIFF_HEADS * 2 * ss, DIFF_V)
        padk = lambda a: jnp.pad(a.reshape(bs, ss, a.shape[-1]),
                                 ((0, 0), (0, new_pad - ss), (0, 0)))
        caches = (cache_ckv[l], cache_krope[l],
                  cache_diff_k[l].reshape(n_pool, page * DIFF_HEADS, DIFF_V),
                  cache_diff_v[l].reshape(n_pool, page * DIFF_HEADS, DIFF_V))
        o_s = _sample_attn_call(page_table, q_s, dq_t, padk(kvc_s), padk(dkb_s), padk(dvb_s),
                                caches, w, lambda_init)
        hs, xn2_s, gates_s = _outproj_router_call(o_s.reshape(n_s, D_MODEL), xs, w, tm_s)
        xs = _moe_call(xn2_s, gates_s, hs, w, g_fin, last, tm_s)
        outs[4].append(ckv_s.reshape(bs, ss, KV_LORA))
        outs[5].append(kr_s.reshape(bs, ss, MLA_ROPE))
        outs[6].append(dk_s.reshape(bs, ss, DIFF_HEADS, 2 * DIFF_QK))
        outs[7].append(dv_s.reshape(bs, ss, DIFF_HEADS, DIFF_V))

    return (xp.reshape(bp, sp, D_MODEL), xs.reshape(bs, ss, D_MODEL),
            *[jnp.stack(o) for o in outs])
```

```python
import functools
import math

import jax
import jax.numpy as jnp
from jax import lax
from jax.experimental import pallas as pl
from jax.experimental.pallas import tpu as pltpu

F32 = jnp.float32
BF16 = jnp.bfloat16

D_MODEL = 1024
MLA_HEADS = 8
MLA_NOPE = 64
MLA_ROPE = 32
MLA_V = 64
Q_LORA = 384
KV_LORA = 256
ROPE_THETA = 10000.0
MLA_SCALE = (MLA_NOPE + MLA_ROPE) ** -0.5
DIFF_HEADS = 4
DIFF_QK = 64
DIFF_V = 2 * DIFF_QK
DIFF_SCALE = DIFF_QK ** -0.5
DIFF_COLS = DIFF_HEADS * DIFF_V
N_EXPERTS = 32
TOP_K = 4
D_FF = D_MODEL
SWIGLU_LIMIT = 7.0
SWIGLU_ALPHA = 1.702
EPS = 1e-6

LANES = 128
ROPE_PAD = LANES
KVC = KV_LORA + ROPE_PAD
IN_PAD_COLS = Q_LORA + KV_LORA + ROPE_PAD + 3 * DIFF_COLS
NEG = -1e30
LOG2E = math.log2(math.e)
VMEM_LIMIT = 56 * 1024 * 1024

_NT = (((1,), (1,)), ((), ()))


def _rms(x, g):
    return x * lax.rsqrt(jnp.mean(x * x, axis=-1, keepdims=True) + EPS) * g


def _alibi_slope(h):
    return 0.25 ** (h + 1) * LOG2E


def _proj_kernel(x_ref, ga_ref, win_ref, gq_ref, wuq_ref, gkv_ref, wuk_ref, tq_ref, tk_ref,
                 ckv_ref, kr_ref, dk_ref, dv_ref, qm_ref, dq_ref, kvc_ref, dkb_ref, dvb_ref):
    tm = x_ref.shape[0]
    xn = _rms(x_ref[...], ga_ref[...])
    h = jnp.dot(xn.astype(BF16), win_ref[...], preferred_element_type=F32)
    o_kv = Q_LORA
    o_kr = o_kv + KV_LORA
    o_dq = o_kr + ROPE_PAD
    o_dk = o_dq + DIFF_COLS
    o_dv = o_dk + DIFF_COLS
    qn = _rms(h[:, :o_kv], gq_ref[...])
    q = jnp.dot(qn.astype(BF16), wuq_ref[...], preferred_element_type=F32)
    n_nope = MLA_HEADS * MLA_NOPE
    n_rope = MLA_HEADS * MLA_ROPE
    tq = tq_ref[...]
    q_rope = (q[:, n_nope:n_nope + n_rope] * tq[:, :n_rope]
              + q[:, n_nope + n_rope:] * tq[:, n_rope:])
    ckv = _rms(h[:, o_kv:o_kr], gkv_ref[...])
    kk = h[:, o_kr:o_dq] * tk_ref[...]
    krope = kk[:, :MLA_ROPE] + kk[:, MLA_ROPE:2 * MLA_ROPE]
    d_k = h[:, o_dk:o_dv]
    d_v = h[:, o_dv:]
    ckv_ref[...] = ckv
    kr_ref[...] = krope
    for hd in range(DIFF_HEADS):
        dk_ref[pl.ds(hd, tm, stride=DIFF_HEADS), :] = d_k[:, DIFF_V * hd:DIFF_V * (hd + 1)]
        dv_ref[pl.ds(hd, tm, stride=DIFF_HEADS), :] = d_v[:, DIFF_V * hd:DIFF_V * (hd + 1)]
    zpad = jnp.zeros((tm, ROPE_PAD - MLA_ROPE), F32)
    kvc_ref[...] = jnp.concatenate([ckv, krope, zpad], axis=-1).astype(BF16)
    dkb_ref[...] = d_k.astype(BF16)
    dvb_ref[...] = d_v.astype(BF16)
    dq_ref[...] = (h[:, o_dq:o_dk] * (DIFF_SCALE * LOG2E)).astype(BF16)
    for p in range(MLA_HEADS // 2):
        ql2 = jnp.dot(q[:, LANES * p:LANES * (p + 1)].astype(BF16), wuk_ref[p],
                      preferred_element_type=F32)
        for u in range(2):
            hh = 2 * p + u
            blk = jnp.concatenate(
                [ql2[:, KV_LORA * u:KV_LORA * (u + 1)],
                 q_rope[:, MLA_ROPE * hh:MLA_ROPE * (hh + 1)], zpad], axis=-1)
            qm_ref[hh] = (blk * (MLA_SCALE * LOG2E)).astype(BF16)


def _proj_call(x2, tq, tk, w, tm):
    n = x2.shape[0]
    ntab = tq.shape[0] // tm
    row = lambda i: (i, 0)
    c2 = lambda i: (0, 0)
    tab = lambda i: (i % ntab, 0)
    in_specs = [
        pl.BlockSpec((tm, D_MODEL), row),
        pl.BlockSpec((1, D_MODEL), c2),
        pl.BlockSpec((D_MODEL, IN_PAD_COLS), c2),
        pl.BlockSpec((1, Q_LORA), c2),
        pl.BlockSpec((Q_LORA, 2 * MLA_HEADS * MLA_NOPE), c2),
        pl.BlockSpec((1, KV_LORA), c2),
        pl.BlockSpec((MLA_HEADS // 2, LANES, 2 * KV_LORA), lambda i: (0, 0, 0)),
        pl.BlockSpec((tm, 2 * MLA_HEADS * MLA_ROPE), tab),
        pl.BlockSpec((tm, ROPE_PAD), tab),
    ]
    out_shape = [
        jax.ShapeDtypeStruct((n, KV_LORA), F32),
        jax.ShapeDtypeStruct((n, MLA_ROPE), F32),
        jax.ShapeDtypeStruct((n * DIFF_HEADS, DIFF_V), F32),
        jax.ShapeDtypeStruct((n * DIFF_HEADS, DIFF_V), F32),
        jax.ShapeDtypeStruct((MLA_HEADS, n, KVC), BF16),
        jax.ShapeDtypeStruct((n, DIFF_COLS), BF16),
        jax.ShapeDtypeStruct((n, KVC), BF16),
        jax.ShapeDtypeStruct((n, DIFF_COLS), BF16),
        jax.ShapeDtypeStruct((n, DIFF_COLS), BF16),
    ]
    out_specs = [
        pl.BlockSpec((tm, KV_LORA), row),
        pl.BlockSpec((tm, MLA_ROPE), row),
        pl.BlockSpec((tm * DIFF_HEADS, DIFF_V), row),
        pl.BlockSpec((tm * DIFF_HEADS, DIFF_V), row),
        pl.BlockSpec((MLA_HEADS, tm, KVC), lambda i: (0, i, 0)),
        pl.BlockSpec((tm, DIFF_COLS), row),
        pl.BlockSpec((tm, KVC), row),
        pl.BlockSpec((tm, DIFF_COLS), row),
        pl.BlockSpec((tm, DIFF_COLS), row),
    ]
    return pl.pallas_call(
        _proj_kernel,
        grid=(n // tm,),
        in_specs=in_specs,
        out_specs=out_specs,
        out_shape=out_shape,
        compiler_params=pltpu.CompilerParams(
            dimension_semantics=("arbitrary",), vmem_limit_bytes=VMEM_LIMIT),
        name="proj",
    )(x2, w["g_attn"], w["w_in"], w["g_q_a"], w["w_uq"], w["g_kv_a"], w["w_uk"], tq, tk)


def _online_update(s, m_ref, l_ref, acc_ref, pv_fn):
    m_prev = m_ref[...]
    m_next = jnp.maximum(m_prev, jnp.max(s, axis=-1, keepdims=True))
    alpha = jnp.exp2(m_prev - m_next)
    p = jnp.exp2(s - m_next[:, :1])
    l_ref[...] = alpha * l_ref[...] + jnp.sum(p, axis=-1, keepdims=True)
    m_ref[...] = m_next
    acc_ref[...] = acc_ref[...] * alpha[:, :1] + pv_fn(p.astype(BF16))


def _lambda_full(lq1_ref, lk1_ref, lq2_ref, lk2_ref, lambda_init):
    a = jnp.sum(lq1_ref[...] * lk1_ref[...], axis=-1, keepdims=True)
    b = jnp.sum(lq2_ref[...] * lk2_ref[...], axis=-1, keepdims=True)
    return jnp.exp(a) - jnp.exp(b) + lambda_init


def _prompt_attn_kernel(qm_ref, dq_ref, kvc_ref, dkb_ref, dvb_ref, wuv_ref, gs_ref,
                        lq1_ref, lk1_ref, lq2_ref, lk2_ref, o_ref,
                        m_a, l_a, acc_a, m_d, l_d, acc_d, *, bq, bk, lambda_init):
    i = pl.program_id(1)
    j = pl.program_id(2)
    last = (i * bq + bq - 1) // bk

    @pl.when(j == 0)
    def _init():
        m_a[...] = jnp.full(m_a.shape, NEG, F32)
        l_a[...] = jnp.zeros(l_a.shape, F32)
        acc_a[...] = jnp.zeros(acc_a.shape, F32)
        m_d[...] = jnp.full(m_d.shape, NEG, F32)
        l_d[...] = jnp.zeros(l_d.shape, F32)
        acc_d[...] = jnp.zeros(acc_d.shape, F32)

    @pl.when(j <= last)
    def _step():
        row = lax.broadcasted_iota(jnp.int32, (bq, bk), 0)
        col = lax.broadcasted_iota(jnp.int32, (bq, bk), 1)
        dist = (i * bq - j * bk) + row - col
        vis = dist >= 0
        distf = dist.astype(F32)

        q = qm_ref[...].reshape(MLA_HEADS * bq, KVC)
        kv = kvc_ref[...]
        s = lax.dot_general(q, kv, _NT, preferred_element_type=F32)
        s = jnp.where(vis[None], s.reshape(MLA_HEADS, bq, bk), NEG).reshape(MLA_HEADS * bq, bk)
        ckv = kv[:, :KV_LORA]
        _online_update(s, m_a, l_a, acc_a,
                       lambda p: jnp.dot(p, ckv, preferred_element_type=F32))

        lane = lax.broadcasted_iota(jnp.int32, (bq, DIFF_V), 1)
        for h in range(DIFF_HEADS):
            qh = dq_ref[:, DIFF_V * h:DIFF_V * (h + 1)]
            zero = jnp.zeros_like(qh)
            qbd = jnp.concatenate([jnp.where(lane < DIFF_QK, qh, zero),
                                   jnp.where(lane >= DIFF_QK, qh, zero)], axis=0)
            kh = dkb_ref[:, DIFF_V * h:DIFF_V * (h + 1)]
            sd = lax.dot_general(qbd, kh, _NT, preferred_element_type=F32)
            bias = jnp.where(vis, -_alibi_slope(h) * distf, NEG)
            sd = (sd.reshape(2, bq, bk) + bias[None]).reshape(2 * bq, bk)
            vh = dvb_ref[:, DIFF_V * h:DIFF_V * (h + 1)]
            _online_update(sd, m_d.at[h], l_d.at[h], acc_d.at[h],
                           lambda p, vh=vh: jnp.dot(p, vh, preferred_element_type=F32))

    @pl.when(j == last)
    def _finalize():
        o_lat = (acc_a[...] / l_a[:, :1]).astype(BF16).reshape(MLA_HEADS, bq, KV_LORA)
        oa = jnp.dot(o_lat[0], wuv_ref[0], preferred_element_type=F32)
        for h in range(1, MLA_HEADS):
            oa = oa + jnp.dot(o_lat[h], wuv_ref[h], preferred_element_type=F32)
        lam = _lambda_full(lq1_ref, lk1_ref, lq2_ref, lk2_ref, lambda_init)
        outs = [oa]
        for h in range(DIFF_HEADS):
            od = acc_d[h] / l_d[h][:, :1]
            ob = od[:bq] - lam * od[bq:]
            outs.append(_rms(ob, gs_ref[...]) * (1.0 - lambda_init))
        o_ref[...] = jnp.concatenate(outs, axis=-1).astype(o_ref.dtype)


def _prompt_attn_call(qm, dq, kvc, dkb, dvb, w, batch, seq, lambda_init, bq=128, bk=1024):
    bk = min(bk, seq)
    n = batch * seq
    nq = seq // bq
    nk = seq // bk
    kv_idx = lambda b, i, j: (b * nk + jnp.minimum(j, (i * bq + bq - 1) // bk), 0)
    q_idx = lambda b, i, j: (b * nq + i, 0)
    c2 = lambda b, i, j: (0, 0)
    in_specs = [
        pl.BlockSpec((MLA_HEADS, bq, KVC), lambda b, i, j: (0, b * nq + i, 0)),
        pl.BlockSpec((bq, DIFF_COLS), q_idx),
        pl.BlockSpec((bk, KVC), kv_idx),
        pl.BlockSpec((bk, DIFF_COLS), kv_idx),
        pl.BlockSpec((bk, DIFF_COLS), kv_idx),
        pl.BlockSpec((MLA_HEADS, KV_LORA, MLA_HEADS * MLA_V), lambda b, i, j: (0, 0, 0)),
        pl.BlockSpec((1, DIFF_V), c2),
        pl.BlockSpec((1, DIFF_QK), c2),
        pl.BlockSpec((1, DIFF_QK), c2),
        pl.BlockSpec((1, DIFF_QK), c2),
        pl.BlockSpec((1, DIFF_QK), c2),
    ]
    scratch = [
        pltpu.VMEM((MLA_HEADS * bq, LANES), F32),
        pltpu.VMEM((MLA_HEADS * bq, LANES), F32),
        pltpu.VMEM((MLA_HEADS * bq, KV_LORA), F32),
        pltpu.VMEM((DIFF_HEADS, 2 * bq, LANES), F32),
        pltpu.VMEM((DIFF_HEADS, 2 * bq, LANES), F32),
        pltpu.VMEM((DIFF_HEADS, 2 * bq, DIFF_V), F32),
    ]
    return pl.pallas_call(
        functools.partial(_prompt_attn_kernel, bq=bq, bk=bk, lambda_init=lambda_init),
        grid=(batch, nq, nk),
        in_specs=in_specs,
        out_specs=pl.BlockSpec((bq, D_MODEL), q_idx),
        out_shape=jax.ShapeDtypeStruct((n, D_MODEL), BF16),
        scratch_shapes=scratch,
        compiler_params=pltpu.CompilerParams(
            dimension_semantics=("arbitrary", "arbitrary", "arbitrary"),
            vmem_limit_bytes=VMEM_LIMIT),
        name="prompt_attn",
    )(qm, dq, kvc, dkb, dvb, w["w_uv_pad"], w["g_subln"],
      w["lambda_q1"], w["lambda_k1"], w["lambda_q2"], w["lambda_k2"])


def _sample_attn_kernel(pt_ref, q_ref, dqt_ref, kvn_ref, dkn_ref, dvn_ref, wuv_ref, gs_ref,
                        lq1_ref, lk1_ref, lq2_ref, lk2_ref, *rest,
                        spb, pps, page, past_len, dec_seq, lambda_init):
    del pt_ref
    n_pg = 4 * pps * spb
    o_ref = rest[n_pg]
    m_a, l_a, acc_a, m_d, l_d, acc_d = rest[n_pg + 1:]
    step = pl.program_id(1)
    nsteps = pl.num_programs(1)
    rows = dec_seq * MLA_HEADS
    hrows = 2 * dec_seq
    assert rows == DIFF_HEADS * hrows

    r1 = lax.broadcasted_iota(jnp.int32, (rows, 1), 0)
    t_a = (r1 // MLA_HEADS).astype(F32)
    t_d = (r1 % dec_seq).astype(F32)
    h_d = r1 // hrows
    slope = jnp.where(h_d == 0, _alibi_slope(0),
                      jnp.where(h_d == 1, _alibi_slope(1),
                                jnp.where(h_d == 2, _alibi_slope(2), _alibi_slope(3))))
    r_q = lax.broadcasted_iota(jnp.int32, (rows, DIFF_V), 0)
    c_q = lax.broadcasted_iota(jnp.int32, (rows, DIFF_V), 1)
    own = (c_q // DIFF_QK) == ((r_q // dec_seq) % 2)
    kpos = (step * (pps * page)
            + lax.broadcasted_iota(jnp.int32, (rows, pps * page), 1)).astype(F32)
    past_bias = slope * ((past_len + t_d) - kpos)
    zk = jnp.zeros((ROPE_PAD - MLA_ROPE, page), F32)

    def head_rows(a, h):
        return a[hrows * h:hrows * (h + 1)]

    def head_page(ref, h):
        return ref[pl.ds(h, page, stride=DIFF_HEADS), :].astype(BF16)

    def seq_refs(g):
        return (m_a.at[g], l_a.at[g], acc_a.at[g], m_d.at[g], l_d.at[g], acc_d.at[g])

    def seq_q(g):
        q = q_ref[g]
        qbd = jnp.where(own, dqt_ref[g], 0.0).astype(BF16)
        return q, qbd

    @pl.when(step == 0)
    def _new_tokens():
        for g in range(spb):
            ma, la, aa, md, ld, ad = seq_refs(g)
            q, qbd = seq_q(g)
            ma[...] = jnp.full(ma.shape, NEG, F32)
            la[...] = jnp.zeros(la.shape, F32)
            aa[...] = jnp.zeros(aa.shape, F32)
            md[...] = jnp.full(md.shape, NEG, F32)
            ld[...] = jnp.zeros(ld.shape, F32)
            ad[...] = jnp.zeros(ad.shape, F32)
            npad = kvn_ref.shape[1]
            c = lax.broadcasted_iota(jnp.int32, (rows, npad), 1).astype(F32)
            kvn = kvn_ref[g]
            s = lax.dot_general(q, kvn, _NT, preferred_element_type=F32)
            s = jnp.where(c <= t_a, s, NEG)
            _online_update(s, ma, la, aa, lambda p, kvn=kvn: jnp.dot(
                p, kvn[:, :KV_LORA], preferred_element_type=F32))
            sd = jnp.concatenate(
                [lax.dot_general(head_rows(qbd, h), dkn_ref[g, :, DIFF_V * h:DIFF_V * (h + 1)],
                                 _NT, preferred_element_type=F32)
                 for h in range(DIFF_HEADS)], axis=0)
            sd = sd + jnp.where(c <= t_d, -slope * (t_d - c), NEG)
            _online_update(sd, md, ld, ad, lambda p, g=g: jnp.concatenate(
                [jnp.dot(head_rows(p, h), dvn_ref[g, :, DIFF_V * h:DIFF_V * (h + 1)],
                         preferred_element_type=F32) for h in range(DIFF_HEADS)], axis=0))

    for g in range(spb):
        pg = rest[4 * pps * g:4 * pps * (g + 1)]
        ckv_refs, kr_refs = pg[0:pps], pg[pps:2 * pps]
        dk_refs, dv_refs = pg[2 * pps:3 * pps], pg[3 * pps:4 * pps]
        ma, la, aa, md, ld, ad = seq_refs(g)
        q, qbd = seq_q(g)
        cks, s_parts = [], []
        sd_parts = [[] for _ in range(DIFF_HEADS)]
        for p in range(pps):
            ck = ckv_refs[p][...].astype(BF16)
            krt = jnp.concatenate([kr_refs[p][...], zk], axis=0).astype(BF16)
            cks.append(ck)
            s_parts.append(
                lax.dot_general(q[:, :KV_LORA], ck, _NT, preferred_element_type=F32)
                + jnp.dot(q[:, KV_LORA:], krt, preferred_element_type=F32))
            for h in range(DIFF_HEADS):
                sd_parts[h].append(lax.dot_general(head_rows(qbd, h), head_page(dk_refs[p], h),
                                                   _NT, preferred_element_type=F32))
        s = jnp.concatenate(s_parts, axis=-1)
        sd = jnp.concatenate([jnp.concatenate(parts, axis=-1) for parts in sd_parts], axis=0)
        sd = sd - past_bias

        def pv_a(p, cks=cks):
            acc = jnp.dot(p[:, :page], cks[0], preferred_element_type=F32)
            for u in range(1, pps):
                acc = acc + jnp.dot(p[:, page * u:page * (u + 1)], cks[u],
                                    preferred_element_type=F32)
            return acc

        def pv_d(p, dv_refs=dv_refs):
            outs = []
            for h in range(DIFF_HEADS):
                ph = head_rows(p, h)
                acc = jnp.dot(ph[:, :page], head_page(dv_refs[0], h), preferred_element_type=F32)
                for u in range(1, pps):
                    acc = acc + jnp.dot(ph[:, page * u:page * (u + 1)],
                                        head_page(dv_refs[u], h), preferred_element_type=F32)
                outs.append(acc)
            return jnp.concatenate(outs, axis=0)

        _online_update(s, ma, la, aa, pv_a)
        _online_update(sd, md, ld, ad, pv_d)

    @pl.when(step == nsteps - 1)
    def _finalize():
        n_v = MLA_HEADS * MLA_V
        r_a = lax.broadcasted_iota(jnp.int32, (rows, n_v), 0)
        c_a = lax.broadcasted_iota(jnp.int32, (rows, n_v), 1)
        lam = _lambda_full(lq1_ref, lk1_ref, lq2_ref, lk2_ref, lambda_init)
        for g in range(spb):
            _, la, aa, _, ld, ad = seq_refs(g)
            o_lat = (aa[...] / la[:, :1]).astype(BF16)
            full = jnp.dot(o_lat, wuv_ref[...], preferred_element_type=F32)
            fa = jnp.where((c_a // MLA_V) == (r_a % MLA_HEADS), full, 0.0)
            oa = jnp.concatenate(
                [jnp.sum(fa[MLA_HEADS * t:MLA_HEADS * (t + 1)], axis=0, keepdims=True)
                 for t in range(dec_seq)], axis=0)
            od = ad[...] / ld[:, :1]
            segs = [oa]
            for h in range(DIFF_HEADS):
                oh = head_rows(od, h)
                ob = oh[:dec_seq] - lam * oh[dec_seq:]
                segs.append(_rms(ob, gs_ref[...]) * (1.0 - lambda_init))
            o_ref[g] = jnp.concatenate(segs, axis=-1)


def _sample_attn_call(page_table, q_s, dq_t, kv_new, dk_new, dv_new, caches, w, lambda_init,
                      pps=8, spb=2):
    cache_ckv, cache_krt, cache_dk, cache_dv = caches
    nb, n_pages = page_table.shape
    page = cache_ckv.shape[1]
    dec_seq = q_s.shape[1] // MLA_HEADS
    rows = q_s.shape[1]
    npad = kv_new.shape[1]
    pps = min(pps, n_pages)
    assert n_pages % pps == 0 and nb % spb == 0
    nsteps = n_pages // pps
    seq3 = lambda b, j, pt: (b, 0, 0)
    c2 = lambda b, j, pt: (0, 0)
    in_specs = [
        pl.BlockSpec((spb, rows, KVC), seq3),
        pl.BlockSpec((spb, rows, DIFF_V), seq3),
        pl.BlockSpec((spb, npad, KVC), seq3),
        pl.BlockSpec((spb, npad, DIFF_COLS), seq3),
        pl.BlockSpec((spb, npad, DIFF_COLS), seq3),
        pl.BlockSpec((KV_LORA, MLA_HEADS * MLA_V), c2),
        pl.BlockSpec((1, DIFF_V), c2),
        pl.BlockSpec((1, DIFF_QK), c2),
        pl.BlockSpec((1, DIFF_QK), c2),
        pl.BlockSpec((1, DIFF_QK), c2),
        pl.BlockSpec((1, DIFF_QK), c2),
    ]
    args = [q_s, dq_t, kv_new, dk_new, dv_new, w["w_uv_all"], w["g_subln"],
            w["lambda_q1"], w["lambda_k1"], w["lambda_q2"], w["lambda_k2"]]
    for g in range(spb):
        for cache in (cache_ckv, cache_krt, cache_dk, cache_dv):
            for p in range(pps):
                in_specs.append(pl.BlockSpec(
                    (None,) + cache.shape[1:],
                    lambda b, j, pt, g=g, p=p: (pt[b * spb + g, j * pps + p], 0, 0)))
                args.append(cache)
    scratch = [
        pltpu.VMEM((spb, rows, LANES), F32),
        pltpu.VMEM((spb, rows, LANES), F32),
        pltpu.VMEM((spb, rows, KV_LORA), F32),
        pltpu.VMEM((spb, rows, LANES), F32),
        pltpu.VMEM((spb, rows, LANES), F32),
        pltpu.VMEM((spb, rows, DIFF_V), F32),
    ]
    grid_spec = pltpu.PrefetchScalarGridSpec(
        num_scalar_prefetch=1,
        grid=(nb // spb, nsteps),
        in_specs=in_specs,
        out_specs=pl.BlockSpec((spb, dec_seq, D_MODEL), seq3),
        scratch_shapes=scratch,
    )
    return pl.pallas_call(
        functools.partial(_sample_attn_kernel, spb=spb, pps=pps, page=page,
                          past_len=float(n_pages * page), dec_seq=dec_seq,
                          lambda_init=lambda_init),
        grid_spec=grid_spec,
        out_shape=jax.ShapeDtypeStruct((nb, dec_seq, D_MODEL), F32),
        compiler_params=pltpu.CompilerParams(
            dimension_semantics=("arbitrary", "arbitrary"), vmem_limit_bytes=VMEM_LIMIT),
        name="sample_attn",
    )(page_table, *args)


def _outproj_router_kernel(o_ref, x_ref, wo_ref, gf_ref, wr_ref, br_ref,
                           hp_ref, xn_ref, ids_ref, rk_ref, gt_ref, cnt_ref, carry):
    tm = x_ref.shape[0]

    @pl.when(pl.program_id(0) == 0)
    def _init():
        carry[...] = jnp.zeros(carry.shape, F32)

    hp = x_ref[...] + jnp.dot(o_ref[...].astype(BF16), wo_ref[...], preferred_element_type=F32)
    hp_ref[...] = hp
    xn = _rms(hp, gf_ref[...])
    xn_ref[...] = xn
    logits = jnp.dot(xn, wr_ref[...], preferred_element_type=F32,
                     precision=lax.Precision.HIGHEST) + br_ref[...]
    lane = lax.broadcasted_iota(jnp.int32, (tm, N_EXPERTS), 1).astype(F32)
    work = logits
    sel = jnp.zeros((tm, N_EXPERTS), F32)
    m1 = None
    hits, idxs = [], []
    for k in range(TOP_K):
        mx = jnp.max(work, axis=-1, keepdims=True)
        if k == 0:
            m1 = mx
        idx = jnp.min(jnp.where(work == mx, lane, float(N_EXPERTS)), axis=-1, keepdims=True)
        hit = lane == idx
        hits.append(hit)
        idxs.append(idx)
        sel = jnp.where(hit, 1.0, sel)
        work = jnp.where(hit, -jnp.inf, work)
    e = jnp.where(sel > 0.0, jnp.exp(logits - m1), 0.0)
    gates = e / jnp.sum(e, axis=-1, keepdims=True)
    r_i = lax.broadcasted_iota(jnp.int32, (tm, tm), 0)
    c_i = lax.broadcasted_iota(jnp.int32, (tm, tm), 1)
    before = jnp.where(r_i > c_i, 1.0, 0.0).astype(BF16)
    rank = carry[...] + jnp.dot(before, sel.astype(BF16), preferred_element_type=F32)
    lane_k = lax.broadcasted_iota(jnp.int32, (tm, TOP_K), 1)
    ids = jnp.zeros((tm, TOP_K), F32)
    rk = jnp.zeros((tm, TOP_K), F32)
    gt = jnp.zeros((tm, TOP_K), F32)
    for k in range(TOP_K):
        rk_k = jnp.sum(jnp.where(hits[k], rank, 0.0), axis=-1, keepdims=True)
        gt_k = jnp.sum(jnp.where(hits[k], gates, 0.0), axis=-1, keepdims=True)
        ids = jnp.where(lane_k == k, idxs[k], ids)
        rk = jnp.where(lane_k == k, rk_k, rk)
        gt = jnp.where(lane_k == k, gt_k, gt)
    ids_ref[...] = ids.astype(jnp.int32)
    rk_ref[...] = rk.astype(jnp.int32)
    gt_ref[...] = gt
    carry[...] = carry[...] + jnp.sum(sel, axis=0, keepdims=True)
    cnt_ref[...] = carry[...]


def _outproj_router_call(o2, x2, w, tm):
    n = x2.shape[0]
    row = lambda i: (i, 0)
    c2 = lambda i: (0, 0)
    return pl.pallas_call(
        _outproj_router_kernel,
        grid=(n // tm,),
        in_specs=[
            pl.BlockSpec((tm, D_MODEL), row),
            pl.BlockSpec((tm, D_MODEL), row),
            pl.BlockSpec((D_MODEL, D_MODEL), c2),
            pl.BlockSpec((1, D_MODEL), c2),
            pl.BlockSpec((D_MODEL, N_EXPERTS), c2),
            pl.BlockSpec((1, N_EXPERTS), c2),
        ],
        out_specs=[
            pl.BlockSpec((tm, D_MODEL), row),
            pl.BlockSpec((tm, D_MODEL), row),
            pl.BlockSpec((tm, TOP_K), row),
            pl.BlockSpec((tm, TOP_K), row),
            pl.BlockSpec((tm, TOP_K), row),
            pl.BlockSpec((1, N_EXPERTS), c2),
        ],
        out_shape=[
            jax.ShapeDtypeStruct((n, D_MODEL), F32),
            jax.ShapeDtypeStruct((n, D_MODEL), F32),
            jax.ShapeDtypeStruct((n, TOP_K), jnp.int32),
            jax.ShapeDtypeStruct((n, TOP_K), jnp.int32),
            jax.ShapeDtypeStruct((n, TOP_K), F32),
            jax.ShapeDtypeStruct((1, N_EXPERTS), F32),
        ],
        scratch_shapes=[pltpu.VMEM((1, N_EXPERTS), F32)],
        compiler_params=pltpu.CompilerParams(
            dimension_semantics=("arbitrary",), vmem_limit_bytes=VMEM_LIMIT),
        name="outproj_router",
    )(o2, x2, w["w_o"], w["g_ffn"], w["w_router"], w["b_router"])


MOE_TM = 256


def _row_dma_loop(n_rows, start_row_copies, unroll=8):
    def body(r, c):
        start_row_copies(r)
        return c
    lax.fori_loop(0, n_rows, body, 0, unroll=unroll)


def _dispatch_kernel(pos_ref, x_ref, init_ref, xs_ref, sem):
    del init_ref
    tm = x_ref.shape[0]

    def start(r):
        for k in range(TOP_K):
            p = pos_ref[TOP_K * r + k]
            pltpu.make_async_copy(x_ref.at[pl.ds(r, 1)], xs_ref.at[pl.ds(p, 1)], sem).start()

    _row_dma_loop(tm, start)
    for k in range(TOP_K):
        pltpu.make_async_copy(x_ref, xs_ref.at[pl.ds(0, tm)], sem).wait()


def _dispatch_call(pos_flat, xn2, n_slots, tm):
    n = xn2.shape[0]
    init = jnp.zeros((n_slots, D_MODEL), F32)
    return pl.pallas_call(
        _dispatch_kernel,
        grid=(n // tm,),
        in_specs=[
            pl.BlockSpec((tm * TOP_K,), lambda i: (i,), memory_space=pltpu.SMEM),
            pl.BlockSpec((tm, D_MODEL), lambda i: (i, 0)),
            pl.BlockSpec(memory_space=pl.ANY),
        ],
        out_specs=pl.BlockSpec(memory_space=pl.ANY),
        out_shape=jax.ShapeDtypeStruct((n_slots, D_MODEL), F32),
        scratch_shapes=[pltpu.SemaphoreType.DMA(())],
        input_output_aliases={2: 0},
        compiler_params=pltpu.CompilerParams(
            dimension_semantics=("arbitrary",), vmem_limit_bytes=VMEM_LIMIT,
            has_side_effects=True),
        name="moe_dispatch",
    )(pos_flat, xn2, init)


def _moe_group_kernel(te_ref, nt_ref, x_ref, wgu_ref, bgu_ref, wd_ref, bd_ref, y_ref):
    del te_ref
    live = pl.program_id(0) < nt_ref[0]

    @pl.when(jnp.logical_not(live))
    def _idle():
        y_ref[...] = jnp.zeros(y_ref.shape, F32)

    @pl.when(live)
    def _tile():
        hgu = jnp.dot(x_ref[...].astype(BF16), wgu_ref[...],
                      preferred_element_type=F32) + bgu_ref[...]
        gate = jnp.minimum(hgu[:, :D_FF], SWIGLU_LIMIT)
        up = jnp.clip(hgu[:, D_FF:], -SWIGLU_LIMIT, SWIGLU_LIMIT)
        act = (up + 1.0) * gate * (1.0 / (1.0 + jnp.exp(-SWIGLU_ALPHA * gate)))
        y_ref[...] = jnp.dot(act.astype(BF16), wd_ref[...],
                             preferred_element_type=F32) + bd_ref[...]


def _moe_group_call(tile_expert, n_tiles, xs, w):
    n_slots = xs.shape[0]
    max_tiles = n_slots // MOE_TM
    live = lambda i, te, nt: jnp.minimum(i, nt[0] - 1)
    wexp = lambda i, te, nt: (te[live(i, te, nt)], 0, 0)
    grid_spec = pltpu.PrefetchScalarGridSpec(
        num_scalar_prefetch=2,
        grid=(max_tiles,),
        in_specs=[
            pl.BlockSpec((MOE_TM, D_MODEL), lambda i, te, nt: (live(i, te, nt), 0)),
            pl.BlockSpec((None, D_MODEL, 2 * D_FF), wexp),
            pl.BlockSpec((None, 1, 2 * D_FF), wexp),
            pl.BlockSpec((None, D_FF, D_MODEL), wexp),
            pl.BlockSpec((None, 1, D_MODEL), wexp),
        ],
        out_specs=pl.BlockSpec((MOE_TM, D_MODEL), lambda i, te, nt: (i, 0)),
    )
    return pl.pallas_call(
        _moe_group_kernel,
        grid_spec=grid_spec,
        out_shape=jax.ShapeDtypeStruct((n_slots, D_MODEL), F32),
        compiler_params=pltpu.CompilerParams(
            dimension_semantics=("arbitrary",), vmem_limit_bytes=VMEM_LIMIT),
        name="moe_group",
    )(tile_expert, n_tiles, xs, w["w_gate_up"], w["b_gate_up"], w["w_down"], w["b_down"])


def _combine_kernel(pos_ref, gt_ref, hp_ref, gfin_ref, ys_ref, y_ref, ybuf, sem, *, final_norm):
    tm = hp_ref.shape[0]

    def start(r):
        for k in range(TOP_K):
            p = pos_ref[TOP_K * r + k]
            pltpu.make_async_copy(ys_ref.at[pl.ds(p, 1)], ybuf.at[k, pl.ds(r, 1)], sem).start()

    _row_dma_loop(tm, start)
    for k in range(TOP_K):
        pltpu.make_async_copy(ys_ref.at[pl.ds(0, tm)], ybuf.at[k], sem).wait()
    gt = gt_ref[...]
    acc = hp_ref[...]
    for k in range(TOP_K):
        acc = acc + gt[:, k:k + 1] * ybuf[k]
    y_ref[...] = _rms(acc, gfin_ref[...]) if final_norm else acc


def _combine_call(pos_flat, gt, hp, g_final, ys, final_norm, tm):
    n = hp.shape[0]
    return pl.pallas_call(
        functools.partial(_combine_kernel, final_norm=final_norm),
        grid=(n // tm,),
        in_specs=[
            pl.BlockSpec((tm * TOP_K,), lambda i: (i,), memory_space=pltpu.SMEM),
            pl.BlockSpec((tm, TOP_K), lambda i: (i, 0)),
            pl.BlockSpec((tm, D_MODEL), lambda i: (i, 0)),
            pl.BlockSpec((1, D_MODEL), lambda i: (0, 0)),
            pl.BlockSpec(memory_space=pl.ANY),
        ],
        out_specs=pl.BlockSpec((tm, D_MODEL), lambda i: (i, 0)),
        out_shape=jax.ShapeDtypeStruct((n, D_MODEL), F32),
        scratch_shapes=[pltpu.VMEM((TOP_K, tm, D_MODEL), F32), pltpu.SemaphoreType.DMA(())],
        compiler_params=pltpu.CompilerParams(
            dimension_semantics=("arbitrary",), vmem_limit_bytes=VMEM_LIMIT),
        name="moe_combine",
    )(pos_flat, gt, hp, g_final, ys)


def _moe(xn2, ids, rk, gt, cnt, hp, w, g_final, final_norm, tm):
    n = xn2.shape[0]
    n_slots = n * TOP_K + N_EXPERTS * MOE_TM
    counts = cnt[0].astype(jnp.int32)
    tiles_e = (counts + (MOE_TM - 1)) // MOE_TM
    tiles_incl = jnp.cumsum(tiles_e)
    offs = (tiles_incl - tiles_e) * MOE_TM
    n_tiles = tiles_incl[-1:]
    tile_ids = jnp.arange(n_slots // MOE_TM, dtype=jnp.int32)
    tile_expert = jnp.minimum(
        jnp.sum((tile_ids[:, None] >= tiles_incl[None, :]).astype(jnp.int32), axis=1),
        N_EXPERTS - 1)
    onehot = ids[..., None] == jnp.arange(N_EXPERTS, dtype=jnp.int32)
    pos = jnp.sum(jnp.where(onehot, offs, 0), axis=-1) + rk
    pos_flat = pos.reshape(n * TOP_K)
    xs = _dispatch_call(pos_flat, xn2, n_slots, tm)
    ys = _moe_group_call(tile_expert, n_tiles, xs, w)
    return _combine_call(pos_flat, gt, hp, g_final, ys, final_norm, tm)


def _rotate_half_cols(wr):
    half = MLA_ROPE // 2
    return jnp.concatenate([-wr[..., half:], wr[..., :half]], axis=-1)


def _prep_layer(l, w_in, g_attn, g_q_a, w_uq, g_kv_a, w_uk, w_uv, lq1, lk1, lq2, lk2, g_subln,
                w_o, g_ffn, w_router, b_router, w_gate_up, b_gate_up, w_down, b_down):
    o_kr = Q_LORA + KV_LORA
    wi = w_in[l]
    k_r = wi[:, o_kr:o_kr + MLA_ROPE]
    w_in_p = jnp.concatenate(
        [wi[:, :o_kr], k_r, _rotate_half_cols(k_r),
         jnp.zeros((D_MODEL, ROPE_PAD - 2 * MLA_ROPE), F32), wi[:, o_kr + MLA_ROPE:]], axis=1)
    wq = w_uq[l].reshape(Q_LORA, MLA_HEADS, MLA_NOPE + MLA_ROPE)
    rope = wq[:, :, MLA_NOPE:]
    w_uq_p = jnp.concatenate(
        [wq[:, :, :MLA_NOPE].reshape(Q_LORA, -1), rope.reshape(Q_LORA, -1),
         _rotate_half_cols(rope).reshape(Q_LORA, -1)], axis=1)
    wk = jnp.transpose(w_uk[l], (1, 2, 0))
    z = jnp.zeros((MLA_NOPE, KV_LORA), F32)
    w_uk_bd = jnp.stack([
        jnp.concatenate([jnp.concatenate([wk[2 * p], z], axis=1),
                         jnp.concatenate([z, wk[2 * p + 1]], axis=1)], axis=0)
        for p in range(MLA_HEADS // 2)])
    wv = jnp.transpose(w_uv[l], (1, 0, 2))
    eye = jnp.eye(MLA_HEADS, dtype=F32)
    w_uv_pad = (wv[:, :, None, :] * eye[:, None, :, None]).reshape(
        MLA_HEADS, KV_LORA, MLA_HEADS * MLA_V)
    w_uv_all = w_uv[l].reshape(KV_LORA, MLA_HEADS * MLA_V)
    return dict(
        g_attn=g_attn[l][None], w_in=w_in_p.astype(BF16), g_q_a=g_q_a[l][None],
        w_uq=w_uq_p.astype(BF16), g_kv_a=g_kv_a[l][None], w_uk=w_uk_bd.astype(BF16),
        w_uv_pad=w_uv_pad.astype(BF16), w_uv_all=w_uv_all.astype(BF16),
        lambda_q1=lq1[l][None], lambda_k1=lk1[l][None], lambda_q2=lq2[l][None],
        lambda_k2=lk2[l][None], g_subln=g_subln[l][None],
        w_o=w_o[l].astype(BF16), g_ffn=g_ffn[l][None], w_router=w_router[l],
        b_router=b_router[l][None], w_gate_up=w_gate_up[l].astype(BF16),
        b_gate_up=b_gate_up[l][:, None, :], w_down=w_down[l].astype(BF16),
        b_down=b_down[l][:, None, :])


def _rope_tables(pos):
    half = MLA_ROPE // 2
    freq = ROPE_THETA ** (-2.0 * jnp.arange(half, dtype=F32) / MLA_ROPE)
    ang = pos[:, None] * freq[None, :]
    cos = jnp.cos(ang)
    sin = jnp.sin(ang)
    cos2 = jnp.concatenate([cos, cos], axis=1)
    sin2 = jnp.concatenate([sin, sin], axis=1)
    tq = jnp.concatenate([jnp.tile(cos2, (1, MLA_HEADS)), jnp.tile(sin2, (1, MLA_HEADS))], axis=1)
    tk = jnp.concatenate(
        [cos2, sin2, jnp.zeros((pos.shape[0], ROPE_PAD - 2 * MLA_ROPE), F32)], axis=1)
    return tq, tk


def _token_tile(n, target):
    t = min(n, target)
    assert n % t == 0
    return t


def kernel(x_prompt, x_sample, cache_ckv, cache_krope, cache_diff_k, cache_diff_v, page_table,
           g_attn, w_in, g_q_a, w_uq, g_kv_a, w_uk, w_uv, lambda_q1, lambda_k1, lambda_q2,
           lambda_k2, g_subln, w_o, g_ffn, w_router, b_router, w_gate_up, b_gate_up, w_down,
           b_down, g_final):
    bp, sp, _ = x_prompt.shape
    bs, ss, _ = x_sample.shape
    depth = w_in.shape[0]
    n_pool, page = cache_ckv.shape[1], cache_ckv.shape[2]
    n_pages = page_table.shape[1]
    past_len = n_pages * page
    n_p = bp * sp
    n_s = bs * ss
    new_pad = LANES

    tq_p, tk_p = _rope_tables(jnp.arange(sp, dtype=F32))
    tq_s, tk_s = _rope_tables(past_len + jnp.arange(ss, dtype=F32))
    tq_s = jnp.tile(tq_s, (bs, 1))
    tk_s = jnp.tile(tk_s, (bs, 1))
    g_fin = g_final[None]

    xp = x_prompt.reshape(n_p, D_MODEL)
    xs = x_sample.reshape(n_s, D_MODEL)
    outs = [[] for _ in range(8)]
    for l in range(depth):
        lambda_init = 0.8 - 0.6 * math.exp(-0.3 * l)
        last = l == depth - 1
        w = _prep_layer(l, w_in, g_attn, g_q_a, w_uq, g_kv_a, w_uk, w_uv, lambda_q1, lambda_k1,
                        lambda_q2, lambda_k2, g_subln, w_o, g_ffn, w_router, b_router,
                        w_gate_up, b_gate_up, w_down, b_down)

        tm_p = _token_tile(sp, 512)
        ckv, kr, dk, dv, qm, dq, kvc, dkb, dvb = _proj_call(xp, tq_p, tk_p, w, tm_p)
        o_p = _prompt_attn_call(qm, dq, kvc, dkb, dvb, w, bp, sp, lambda_init)
        hp, xn2, ids, rk, gt, cnt = _outproj_router_call(o_p, xp, w, tm_p)
        xp = _moe(xn2, ids, rk, gt, cnt, hp, w, g_fin, last, tm_p)
        outs[0].append(ckv.reshape(bp, sp, KV_LORA))
        outs[1].append(kr.reshape(bp, sp, MLA_ROPE))
        outs[2].append(dk.reshape(bp, sp, DIFF_HEADS, 2 * DIFF_QK))
        outs[3].append(dv.reshape(bp, sp, DIFF_HEADS, DIFF_V))

        tm_s = _token_tile(n_s, 512)
        ckv_s, kr_s, dk_s, dv_s, qm_s, dq_s, kvc_s, dkb_s, dvb_s = _proj_call(
            xs, tq_s, tk_s, w, tm_s)
        q_s = qm_s.reshape(MLA_HEADS, bs, ss, KVC).transpose(1, 2, 0, 3).reshape(
            bs, ss * MLA_HEADS, KVC)
        dq_t = jnp.broadcast_to(
            dq_s.astype(F32).reshape(bs, ss, DIFF_HEADS, 1, DIFF_V).transpose(0, 2, 3, 1, 4),
            (bs, DIFF_HEADS, 2, ss, DIFF_V)).reshape(bs, DIFF_HEADS * 2 * ss, DIFF_V)
        padk = lambda a: jnp.pad(a.reshape(bs, ss, a.shape[-1]),
                                 ((0, 0), (0, new_pad - ss), (0, 0)))
        caches = (cache_ckv[l], jnp.swapaxes(cache_krope[l], 1, 2),
                  cache_diff_k[l].reshape(n_pool, page * DIFF_HEADS, DIFF_V),
                  cache_diff_v[l].reshape(n_pool, page * DIFF_HEADS, DIFF_V))
        o_s = _sample_attn_call(page_table, q_s, dq_t, padk(kvc_s), padk(dkb_s), padk(dvb_s),
                                caches, w, lambda_init)
        hs, xn2_s, ids_s, rk_s, gt_s, cnt_s = _outproj_router_call(
            o_s.reshape(n_s, D_MODEL), xs, w, tm_s)
        xs = _moe(xn2_s, ids_s, rk_s, gt_s, cnt_s, hs, w, g_fin, last, tm_s)
        outs[4].append(ckv_s.reshape(bs, ss, KV_LORA))
        outs[5].append(kr_s.reshape(bs, ss, MLA_ROPE))
        outs[6].append(dk_s.reshape(bs, ss, DIFF_HEADS, 2 * DIFF_QK))
        outs[7].append(dv_s.reshape(bs, ss, DIFF_HEADS, DIFF_V))

    return (xp.reshape(bp, sp, D_MODEL), xs.reshape(bs, ss, D_MODEL),
            *[jnp.stack(o) for o in outs])
```

```python
import functools
import math

import jax
import jax.numpy as jnp
from jax import lax
from jax.experimental import pallas as pl
from jax.experimental.pallas import tpu as pltpu

F32 = jnp.float32
BF16 = jnp.bfloat16

D_MODEL = 1024
MLA_HEADS = 8
MLA_NOPE = 64
MLA_ROPE = 32
MLA_V = 64
Q_LORA = 384
KV_LORA = 256
ROPE_THETA = 10000.0
MLA_SCALE = (MLA_NOPE + MLA_ROPE) ** -0.5
DIFF_HEADS = 4
DIFF_QK = 64
DIFF_V = 2 * DIFF_QK
DIFF_SCALE = DIFF_QK ** -0.5
DIFF_COLS = DIFF_HEADS * DIFF_V
N_EXPERTS = 32
TOP_K = 4
D_FF = D_MODEL
SWIGLU_LIMIT = 7.0
SWIGLU_ALPHA = 1.702
EPS = 1e-6

LANES = 128
ROPE_PAD = LANES
KVC = KV_LORA + ROPE_PAD
IN_PAD_COLS = Q_LORA + KV_LORA + ROPE_PAD + 3 * DIFF_COLS
NEG = -1e30
LOG2E = math.log2(math.e)
VMEM_LIMIT = 56 * 1024 * 1024

_NT = (((1,), (1,)), ((), ()))


def _rms(x, g):
    return x * lax.rsqrt(jnp.mean(x * x, axis=-1, keepdims=True) + EPS) * g


def _alibi_slope(h):
    return 0.25 ** (h + 1) * LOG2E


def _proj_kernel(x_ref, ga_ref, win_ref, gq_ref, wuq_ref, gkv_ref, wuk_ref, tq_ref, tk_ref,
                 ckv_ref, kr_ref, dk_ref, dv_ref, qm_ref, dq_ref, kvc_ref, dkb_ref, dvb_ref):
    tm = x_ref.shape[0]
    xn = _rms(x_ref[...], ga_ref[...])
    h = jnp.dot(xn.astype(BF16), win_ref[...], preferred_element_type=F32)
    o_kv = Q_LORA
    o_kr = o_kv + KV_LORA
    o_dq = o_kr + ROPE_PAD
    o_dk = o_dq + DIFF_COLS
    o_dv = o_dk + DIFF_COLS
    qn = _rms(h[:, :o_kv], gq_ref[...])
    q = jnp.dot(qn.astype(BF16), wuq_ref[...], preferred_element_type=F32)
    n_nope = MLA_HEADS * MLA_NOPE
    n_rope = MLA_HEADS * MLA_ROPE
    tq = tq_ref[...]
    q_rope = (q[:, n_nope:n_nope + n_rope] * tq[:, :n_rope]
              + q[:, n_nope + n_rope:] * tq[:, n_rope:])
    ckv = _rms(h[:, o_kv:o_kr], gkv_ref[...])
    kk = h[:, o_kr:o_dq] * tk_ref[...]
    krope = kk[:, :MLA_ROPE] + kk[:, MLA_ROPE:2 * MLA_ROPE]
    d_k = h[:, o_dk:o_dv]
    d_v = h[:, o_dv:]
    ckv_ref[...] = ckv
    kr_ref[...] = krope
    for hd in range(DIFF_HEADS):
        dk_ref[pl.ds(hd, tm, stride=DIFF_HEADS), :] = d_k[:, DIFF_V * hd:DIFF_V * (hd + 1)]
        dv_ref[pl.ds(hd, tm, stride=DIFF_HEADS), :] = d_v[:, DIFF_V * hd:DIFF_V * (hd + 1)]
    zpad = jnp.zeros((tm, ROPE_PAD - MLA_ROPE), F32)
    kvc_ref[...] = jnp.concatenate([ckv, krope, zpad], axis=-1).astype(BF16)
    dkb_ref[...] = d_k.astype(BF16)
    dvb_ref[...] = d_v.astype(BF16)
    dq_ref[...] = (h[:, o_dq:o_dk] * (DIFF_SCALE * LOG2E)).astype(BF16)
    for p in range(MLA_HEADS // 2):
        ql2 = jnp.dot(q[:, LANES * p:LANES * (p + 1)].astype(BF16), wuk_ref[p],
                      preferred_element_type=F32)
        for u in range(2):
            hh = 2 * p + u
            blk = jnp.concatenate(
                [ql2[:, KV_LORA * u:KV_LORA * (u + 1)],
                 q_rope[:, MLA_ROPE * hh:MLA_ROPE * (hh + 1)], zpad], axis=-1)
            qm_ref[hh] = (blk * (MLA_SCALE * LOG2E)).astype(BF16)


def _proj_call(x2, tq, tk, w, tm):
    n = x2.shape[0]
    ntab = tq.shape[0] // tm
    row = lambda i: (i, 0)
    c2 = lambda i: (0, 0)
    tab = lambda i: (i % ntab, 0)
    in_specs = [
        pl.BlockSpec((tm, D_MODEL), row),
        pl.BlockSpec((1, D_MODEL), c2),
        pl.BlockSpec((D_MODEL, IN_PAD_COLS), c2),
        pl.BlockSpec((1, Q_LORA), c2),
        pl.BlockSpec((Q_LORA, 2 * MLA_HEADS * MLA_NOPE), c2),
        pl.BlockSpec((1, KV_LORA), c2),
        pl.BlockSpec((MLA_HEADS // 2, LANES, 2 * KV_LORA), lambda i: (0, 0, 0)),
        pl.BlockSpec((tm, 2 * MLA_HEADS * MLA_ROPE), tab),
        pl.BlockSpec((tm, ROPE_PAD), tab),
    ]
    out_shape = [
        jax.ShapeDtypeStruct((n, KV_LORA), F32),
        jax.ShapeDtypeStruct((n, MLA_ROPE), F32),
        jax.ShapeDtypeStruct((n * DIFF_HEADS, DIFF_V), F32),
        jax.ShapeDtypeStruct((n * DIFF_HEADS, DIFF_V), F32),
        jax.ShapeDtypeStruct((MLA_HEADS, n, KVC), BF16),
        jax.ShapeDtypeStruct((n, DIFF_COLS), BF16),
        jax.ShapeDtypeStruct((n, KVC), BF16),
        jax.ShapeDtypeStruct((n, DIFF_COLS), BF16),
        jax.ShapeDtypeStruct((n, DIFF_COLS), BF16),
    ]
    out_specs = [
        pl.BlockSpec((tm, KV_LORA), row),
        pl.BlockSpec((tm, MLA_ROPE), row),
        pl.BlockSpec((tm * DIFF_HEADS, DIFF_V), row),
        pl.BlockSpec((tm * DIFF_HEADS, DIFF_V), row),
        pl.BlockSpec((MLA_HEADS, tm, KVC), lambda i: (0, i, 0)),
        pl.BlockSpec((tm, DIFF_COLS), row),
        pl.BlockSpec((tm, KVC), row),
        pl.BlockSpec((tm, DIFF_COLS), row),
        pl.BlockSpec((tm, DIFF_COLS), row),
    ]
    return pl.pallas_call(
        _proj_kernel,
        grid=(n // tm,),
        in_specs=in_specs,
        out_specs=out_specs,
        out_shape=out_shape,
        compiler_params=pltpu.CompilerParams(
            dimension_semantics=("arbitrary",), vmem_limit_bytes=VMEM_LIMIT),
        name="proj",
    )(x2, w["g_attn"], w["w_in"], w["g_q_a"], w["w_uq"], w["g_kv_a"], w["w_uk"], tq, tk)


def _online_update(s, m_ref, l_ref, acc_ref, pv_fn):
    m_prev = m_ref[...]
    m_next = jnp.maximum(m_prev, jnp.max(s, axis=-1, keepdims=True))
    alpha = jnp.exp2(m_prev - m_next)
    wide = lambda a, n: a if n == LANES else jnp.concatenate([a] * (n // LANES), axis=-1)
    p = jnp.exp2(s - wide(m_next, s.shape[-1]))
    l_ref[...] = alpha * l_ref[...] + jnp.sum(p, axis=-1, keepdims=True)
    m_ref[...] = m_next
    acc_ref[...] = acc_ref[...] * wide(alpha, acc_ref.shape[-1]) + pv_fn(p.astype(BF16))


def _lambda_full(lq1_ref, lk1_ref, lq2_ref, lk2_ref, lambda_init):
    a = jnp.sum(lq1_ref[...] * lk1_ref[...], axis=-1, keepdims=True)
    b = jnp.sum(lq2_ref[...] * lk2_ref[...], axis=-1, keepdims=True)
    return jnp.exp(a) - jnp.exp(b) + lambda_init


def _prompt_attn_kernel(qm_ref, dq_ref, kvc_ref, dkb_ref, dvb_ref, wuv_ref, gs_ref,
                        lq1_ref, lk1_ref, lq2_ref, lk2_ref, o_ref,
                        m_a, l_a, acc_a, m_d, l_d, acc_d, *, bq, bk, lambda_init):
    i = pl.program_id(1)
    j = pl.program_id(2)
    last = (i * bq + bq - 1) // bk

    @pl.when(j == 0)
    def _init():
        m_a[...] = jnp.full(m_a.shape, NEG, F32)
        l_a[...] = jnp.zeros(l_a.shape, F32)
        acc_a[...] = jnp.zeros(acc_a.shape, F32)
        m_d[...] = jnp.full(m_d.shape, NEG, F32)
        l_d[...] = jnp.zeros(l_d.shape, F32)
        acc_d[...] = jnp.zeros(acc_d.shape, F32)

    @pl.when(j <= last)
    def _step():
        row = lax.broadcasted_iota(jnp.int32, (bq, bk), 0)
        col = lax.broadcasted_iota(jnp.int32, (bq, bk), 1)
        dist = (i * bq - j * bk) + row - col
        vis = dist >= 0
        distf = dist.astype(F32)

        q = qm_ref[...].reshape(MLA_HEADS * bq, KVC)
        kv = kvc_ref[...]
        s = lax.dot_general(q, kv, _NT, preferred_element_type=F32)
        mask_bias = jnp.where(vis, 0.0, NEG)
        s = (s.reshape(MLA_HEADS, bq, bk) + mask_bias[None]).reshape(MLA_HEADS * bq, bk)
        ckv = kv[:, :KV_LORA]
        _online_update(s, m_a, l_a, acc_a,
                       lambda p: jnp.dot(p, ckv, preferred_element_type=F32))

        lane = lax.broadcasted_iota(jnp.int32, (bq, DIFF_V), 1)
        for h in range(DIFF_HEADS):
            qh = dq_ref[:, DIFF_V * h:DIFF_V * (h + 1)]
            zero = jnp.zeros_like(qh)
            qbd = jnp.concatenate([jnp.where(lane < DIFF_QK, qh, zero),
                                   jnp.where(lane >= DIFF_QK, qh, zero)], axis=0)
            kh = dkb_ref[:, DIFF_V * h:DIFF_V * (h + 1)]
            sd = lax.dot_general(qbd, kh, _NT, preferred_element_type=F32)
            bias = mask_bias - _alibi_slope(h) * distf
            sd = (sd.reshape(2, bq, bk) + bias[None]).reshape(2 * bq, bk)
            vh = dvb_ref[:, DIFF_V * h:DIFF_V * (h + 1)]
            _online_update(sd, m_d.at[h], l_d.at[h], acc_d.at[h],
                           lambda p, vh=vh: jnp.dot(p, vh, preferred_element_type=F32))

    @pl.when(j == last)
    def _finalize():
        o_lat = (acc_a[...] / l_a[:, :1]).astype(BF16).reshape(MLA_HEADS, bq, KV_LORA)
        oa = jnp.dot(o_lat[0], wuv_ref[0], preferred_element_type=F32)
        for h in range(1, MLA_HEADS):
            oa = oa + jnp.dot(o_lat[h], wuv_ref[h], preferred_element_type=F32)
        lam = _lambda_full(lq1_ref, lk1_ref, lq2_ref, lk2_ref, lambda_init)
        outs = [oa]
        for h in range(DIFF_HEADS):
            od = acc_d[h] / l_d[h][:, :1]
            ob = od[:bq] - lam * od[bq:]
            outs.append(_rms(ob, gs_ref[...]) * (1.0 - lambda_init))
        o_ref[...] = jnp.concatenate(outs, axis=-1).astype(o_ref.dtype)


def _prompt_attn_call(qm, dq, kvc, dkb, dvb, w, batch, seq, lambda_init, bq=128, bk=1024):
    bk = min(bk, seq)
    n = batch * seq
    nq = seq // bq
    nk = seq // bk
    kv_idx = lambda b, i, j: (b * nk + jnp.minimum(j, (i * bq + bq - 1) // bk), 0)
    q_idx = lambda b, i, j: (b * nq + i, 0)
    c2 = lambda b, i, j: (0, 0)
    in_specs = [
        pl.BlockSpec((MLA_HEADS, bq, KVC), lambda b, i, j: (0, b * nq + i, 0)),
        pl.BlockSpec((bq, DIFF_COLS), q_idx),
        pl.BlockSpec((bk, KVC), kv_idx),
        pl.BlockSpec((bk, DIFF_COLS), kv_idx),
        pl.BlockSpec((bk, DIFF_COLS), kv_idx),
        pl.BlockSpec((MLA_HEADS, KV_LORA, MLA_HEADS * MLA_V), lambda b, i, j: (0, 0, 0)),
        pl.BlockSpec((1, DIFF_V), c2),
        pl.BlockSpec((1, DIFF_QK), c2),
        pl.BlockSpec((1, DIFF_QK), c2),
        pl.BlockSpec((1, DIFF_QK), c2),
        pl.BlockSpec((1, DIFF_QK), c2),
    ]
    scratch = [
        pltpu.VMEM((MLA_HEADS * bq, LANES), F32),
        pltpu.VMEM((MLA_HEADS * bq, LANES), F32),
        pltpu.VMEM((MLA_HEADS * bq, KV_LORA), F32),
        pltpu.VMEM((DIFF_HEADS, 2 * bq, LANES), F32),
        pltpu.VMEM((DIFF_HEADS, 2 * bq, LANES), F32),
        pltpu.VMEM((DIFF_HEADS, 2 * bq, DIFF_V), F32),
    ]
    return pl.pallas_call(
        functools.partial(_prompt_attn_kernel, bq=bq, bk=bk, lambda_init=lambda_init),
        grid=(batch, nq, nk),
        in_specs=in_specs,
        out_specs=pl.BlockSpec((bq, D_MODEL), q_idx),
        out_shape=jax.ShapeDtypeStruct((n, D_MODEL), BF16),
        scratch_shapes=scratch,
        compiler_params=pltpu.CompilerParams(
            dimension_semantics=("arbitrary", "arbitrary", "arbitrary"),
            vmem_limit_bytes=VMEM_LIMIT),
        name="prompt_attn",
    )(qm, dq, kvc, dkb, dvb, w["w_uv_pad"], w["g_subln"],
      w["lambda_q1"], w["lambda_k1"], w["lambda_q2"], w["lambda_k2"])


def _sample_attn_kernel(pt_ref, q_ref, dqt_ref, kvn_ref, dkn_ref, dvn_ref, wuv_ref, gs_ref,
                        lq1_ref, lk1_ref, lq2_ref, lk2_ref, *rest,
                        spb, pps, page, past_len, dec_seq, lambda_init):
    del pt_ref
    n_pg = 4 * pps * spb
    o_ref = rest[n_pg]
    m_a, l_a, acc_a, m_d, l_d, acc_d = rest[n_pg + 1:]
    step = pl.program_id(1)
    nsteps = pl.num_programs(1)
    rows = dec_seq * MLA_HEADS
    hrows = 2 * dec_seq
    assert rows == DIFF_HEADS * hrows

    r1 = lax.broadcasted_iota(jnp.int32, (rows, 1), 0)
    t_a = (r1 // MLA_HEADS).astype(F32)
    t_d = (r1 % dec_seq).astype(F32)
    h_d = r1 // hrows
    slope = jnp.where(h_d == 0, _alibi_slope(0),
                      jnp.where(h_d == 1, _alibi_slope(1),
                                jnp.where(h_d == 2, _alibi_slope(2), _alibi_slope(3))))
    r_q = lax.broadcasted_iota(jnp.int32, (rows, DIFF_V), 0)
    c_q = lax.broadcasted_iota(jnp.int32, (rows, DIFF_V), 1)
    own = (c_q // DIFF_QK) == ((r_q // dec_seq) % 2)
    kpos = (step * (pps * page)
            + lax.broadcasted_iota(jnp.int32, (rows, pps * page), 1)).astype(F32)
    past_bias = slope * ((past_len + t_d) - kpos)
    zk = jnp.zeros((ROPE_PAD - MLA_ROPE, page), F32)

    def head_rows(a, h):
        return a[hrows * h:hrows * (h + 1)]

    def pair_rows(a, pr):
        return a[2 * hrows * pr:2 * hrows * (pr + 1)]

    def pair_page(ref, pr):
        return jnp.concatenate(
            [ref[pl.ds(2 * pr + u, page, stride=DIFF_HEADS), :].astype(BF16) for u in range(2)],
            axis=-1)

    def seq_refs(g):
        return (m_a.at[g], l_a.at[g], acc_a.at[g], m_d.at[g], l_d.at[g], acc_d.at[g])

    def seq_q(g):
        q = q_ref[g]
        qh = jnp.where(own, dqt_ref[g], 0.0).astype(BF16)
        zq = jnp.zeros((hrows, DIFF_V), BF16)
        qbd = jnp.concatenate(
            [jnp.concatenate([jnp.concatenate([head_rows(qh, 2 * pr), zq], axis=-1),
                              jnp.concatenate([zq, head_rows(qh, 2 * pr + 1)], axis=-1)], axis=0)
             for pr in range(DIFF_HEADS // 2)], axis=0)
        return q, qbd

    @pl.when(step == 0)
    def _new_tokens():
        for g in range(spb):
            ma, la, aa, md, ld, ad = seq_refs(g)
            q, qbd = seq_q(g)
            ma[...] = jnp.full(ma.shape, NEG, F32)
            la[...] = jnp.zeros(la.shape, F32)
            aa[...] = jnp.zeros(aa.shape, F32)
            md[...] = jnp.full(md.shape, NEG, F32)
            ld[...] = jnp.zeros(ld.shape, F32)
            ad[...] = jnp.zeros(ad.shape, F32)
            npad = kvn_ref.shape[1]
            c = lax.broadcasted_iota(jnp.int32, (rows, npad), 1).astype(F32)
            kvn = kvn_ref[g]
            s = lax.dot_general(q, kvn, _NT, preferred_element_type=F32)
            s = jnp.where(c <= t_a, s, NEG)
            _online_update(s, ma, la, aa, lambda p, kvn=kvn: jnp.dot(
                p, kvn[:, :KV_LORA], preferred_element_type=F32))
            sd = jnp.concatenate(
                [lax.dot_general(pair_rows(qbd, pr),
                                 dkn_ref[g, :, 2 * DIFF_V * pr:2 * DIFF_V * (pr + 1)],
                                 _NT, preferred_element_type=F32)
                 for pr in range(DIFF_HEADS // 2)], axis=0)
            sd = sd + jnp.where(c <= t_d, -slope * (t_d - c), NEG)
            _online_update(sd, md, ld, ad, lambda p, g=g: jnp.concatenate(
                [jnp.dot(pair_rows(p, pr), dvn_ref[g, :, 2 * DIFF_V * pr:2 * DIFF_V * (pr + 1)],
                         preferred_element_type=F32) for pr in range(DIFF_HEADS // 2)], axis=0))

    for g in range(spb):
        pg = rest[4 * pps * g:4 * pps * (g + 1)]
        ckv_refs, kr_refs = pg[0:pps], pg[pps:2 * pps]
        dk_refs, dv_refs = pg[2 * pps:3 * pps], pg[3 * pps:4 * pps]
        ma, la, aa, md, ld, ad = seq_refs(g)
        q, qbd = seq_q(g)
        cks, s_parts = [], []
        sd_parts = [[] for _ in range(DIFF_HEADS // 2)]
        for p in range(pps):
            ck = ckv_refs[p][...].astype(BF16)
            krt = jnp.concatenate([kr_refs[p][...], zk], axis=0).astype(BF16)
            cks.append(ck)
            s_parts.append(
                lax.dot_general(q[:, :KV_LORA], ck, _NT, preferred_element_type=F32)
                + jnp.dot(q[:, KV_LORA:], krt, preferred_element_type=F32))
            for pr in range(DIFF_HEADS // 2):
                sd_parts[pr].append(lax.dot_general(pair_rows(qbd, pr), pair_page(dk_refs[p], pr),
                                                    _NT, preferred_element_type=F32))
        s = jnp.concatenate(s_parts, axis=-1)
        sd = jnp.concatenate([jnp.concatenate(parts, axis=-1) for parts in sd_parts], axis=0)
        sd = sd - past_bias

        def pv_a(p, cks=cks):
            acc = jnp.dot(p[:, :page], cks[0], preferred_element_type=F32)
            for u in range(1, pps):
                acc = acc + jnp.dot(p[:, page * u:page * (u + 1)], cks[u],
                                    preferred_element_type=F32)
            return acc

        def pv_d(p, dv_refs=dv_refs):
            outs = []
            for pr in range(DIFF_HEADS // 2):
                ph = pair_rows(p, pr)
                acc = jnp.dot(ph[:, :page], pair_page(dv_refs[0], pr), preferred_element_type=F32)
                for u in range(1, pps):
                    acc = acc + jnp.dot(ph[:, page * u:page * (u + 1)],
                                        pair_page(dv_refs[u], pr), preferred_element_type=F32)
                outs.append(acc)
            return jnp.concatenate(outs, axis=0)

        _online_update(s, ma, la, aa, pv_a)
        _online_update(sd, md, ld, ad, pv_d)

    @pl.when(step == nsteps - 1)
    def _finalize():
        n_v = MLA_HEADS * MLA_V
        r_a = lax.broadcasted_iota(jnp.int32, (rows, n_v), 0)
        c_a = lax.broadcasted_iota(jnp.int32, (rows, n_v), 1)
        lam = _lambda_full(lq1_ref, lk1_ref, lq2_ref, lk2_ref, lambda_init)
        for g in range(spb):
            _, la, aa, _, ld, ad = seq_refs(g)
            o_lat = (aa[...] / la[:, :1]).astype(BF16)
            full = jnp.dot(o_lat, wuv_ref[...], preferred_element_type=F32)
            fa = jnp.where((c_a // MLA_V) == (r_a % MLA_HEADS), full, 0.0)
            oa = jnp.concatenate(
                [jnp.sum(fa[MLA_HEADS * t:MLA_HEADS * (t + 1)], axis=0, keepdims=True)
                 for t in range(dec_seq)], axis=0)
            od = ad[...] / ld[:, :1]
            segs = [oa]
            for h in range(DIFF_HEADS):
                oh = head_rows(od, h)[:, DIFF_V * (h % 2):DIFF_V * (h % 2 + 1)]
                ob = oh[:dec_seq] - lam * oh[dec_seq:]
                segs.append(_rms(ob, gs_ref[...]) * (1.0 - lambda_init))
            o_ref[g] = jnp.concatenate(segs, axis=-1)


def _sample_attn_call(page_table, q_s, dq_t, kv_new, dk_new, dv_new, caches, w, lambda_init,
                      pps=8, spb=2):
    cache_ckv, cache_krt, cache_dk, cache_dv = caches
    nb, n_pages = page_table.shape
    page = cache_ckv.shape[1]
    dec_seq = q_s.shape[1] // MLA_HEADS
    rows = q_s.shape[1]
    npad = kv_new.shape[1]
    pps = min(pps, n_pages)
    assert n_pages % pps == 0 and nb % spb == 0
    nsteps = n_pages // pps
    seq3 = lambda b, j, pt: (b, 0, 0)
    c2 = lambda b, j, pt: (0, 0)
    in_specs = [
        pl.BlockSpec((spb, rows, KVC), seq3),
        pl.BlockSpec((spb, rows, DIFF_V), seq3),
        pl.BlockSpec((spb, npad, KVC), seq3),
        pl.BlockSpec((spb, npad, DIFF_COLS), seq3),
        pl.BlockSpec((spb, npad, DIFF_COLS), seq3),
        pl.BlockSpec((KV_LORA, MLA_HEADS * MLA_V), c2),
        pl.BlockSpec((1, DIFF_V), c2),
        pl.BlockSpec((1, DIFF_QK), c2),
        pl.BlockSpec((1, DIFF_QK), c2),
        pl.BlockSpec((1, DIFF_QK), c2),
        pl.BlockSpec((1, DIFF_QK), c2),
    ]
    args = [q_s, dq_t, kv_new, dk_new, dv_new, w["w_uv_all"], w["g_subln"],
            w["lambda_q1"], w["lambda_k1"], w["lambda_q2"], w["lambda_k2"]]
    for g in range(spb):
        for cache in (cache_ckv, cache_krt, cache_dk, cache_dv):
            for p in range(pps):
                in_specs.append(pl.BlockSpec(
                    (None,) + cache.shape[1:],
                    lambda b, j, pt, g=g, p=p: (pt[b * spb + g, j * pps + p], 0, 0)))
                args.append(cache)
    scratch = [
        pltpu.VMEM((spb, rows, LANES), F32),
        pltpu.VMEM((spb, rows, LANES), F32),
        pltpu.VMEM((spb, rows, KV_LORA), F32),
        pltpu.VMEM((spb, rows, LANES), F32),
        pltpu.VMEM((spb, rows, LANES), F32),
        pltpu.VMEM((spb, rows, 2 * DIFF_V), F32),
    ]
    grid_spec = pltpu.PrefetchScalarGridSpec(
        num_scalar_prefetch=1,
        grid=(nb // spb, nsteps),
        in_specs=in_specs,
        out_specs=pl.BlockSpec((spb, dec_seq, D_MODEL), seq3),
        scratch_shapes=scratch,
    )
    return pl.pallas_call(
        functools.partial(_sample_attn_kernel, spb=spb, pps=pps, page=page,
                          past_len=float(n_pages * page), dec_seq=dec_seq,
                          lambda_init=lambda_init),
        grid_spec=grid_spec,
        out_shape=jax.ShapeDtypeStruct((nb, dec_seq, D_MODEL), F32),
        compiler_params=pltpu.CompilerParams(
            dimension_semantics=("arbitrary", "arbitrary"), vmem_limit_bytes=VMEM_LIMIT),
        name="sample_attn",
    )(page_table, *args)


def _outproj_router_kernel(o_ref, x_ref, wo_ref, gf_ref, wr_ref, br_ref,
                           hp_ref, xn_ref, ids_ref, rk_ref, gt_ref, cnt_ref, carry):
    tm = x_ref.shape[0]

    @pl.when(pl.program_id(0) == 0)
    def _init():
        carry[...] = jnp.zeros(carry.shape, F32)

    hp = x_ref[...] + jnp.dot(o_ref[...].astype(BF16), wo_ref[...], preferred_element_type=F32)
    hp_ref[...] = hp
    xn = _rms(hp, gf_ref[...])
    xn_ref[...] = xn
    logits = jnp.dot(xn, wr_ref[...], preferred_element_type=F32,
                     precision=lax.Precision.HIGHEST) + br_ref[...]
    lane = lax.broadcasted_iota(jnp.int32, (tm, N_EXPERTS), 1).astype(F32)
    work = logits
    sel = jnp.zeros((tm, N_EXPERTS), F32)
    m1 = None
    hits, idxs = [], []
    for k in range(TOP_K):
        mx = jnp.max(work, axis=-1, keepdims=True)
        if k == 0:
            m1 = mx
        idx = jnp.min(jnp.where(work == mx, lane, float(N_EXPERTS)), axis=-1, keepdims=True)
        hit = lane == idx
        hits.append(hit)
        idxs.append(idx)
        sel = jnp.where(hit, 1.0, sel)
        work = jnp.where(hit, -jnp.inf, work)
    e = jnp.where(sel > 0.0, jnp.exp(logits - m1), 0.0)
    gates = e / jnp.sum(e, axis=-1, keepdims=True)
    r_i = lax.broadcasted_iota(jnp.int32, (tm, tm), 0)
    c_i = lax.broadcasted_iota(jnp.int32, (tm, tm), 1)
    before = jnp.where(r_i > c_i, 1.0, 0.0).astype(BF16)
    rank = carry[...] + jnp.dot(before, sel.astype(BF16), preferred_element_type=F32)
    lane_k = lax.broadcasted_iota(jnp.int32, (tm, TOP_K), 1)
    ids = jnp.zeros((tm, TOP_K), F32)
    rk = jnp.zeros((tm, TOP_K), F32)
    gt = jnp.zeros((tm, TOP_K), F32)
    for k in range(TOP_K):
        rk_k = jnp.sum(jnp.where(hits[k], rank, 0.0), axis=-1, keepdims=True)
        gt_k = jnp.sum(jnp.where(hits[k], gates, 0.0), axis=-1, keepdims=True)
        ids = jnp.where(lane_k == k, idxs[k], ids)
        rk = jnp.where(lane_k == k, rk_k, rk)
        gt = jnp.where(lane_k == k, gt_k, gt)
    ids_ref[...] = ids.astype(jnp.int32)
    rk_ref[...] = rk.astype(jnp.int32)
    gt_ref[...] = gt
    carry[...] = carry[...] + jnp.sum(sel, axis=0, keepdims=True)
    cnt_ref[...] = carry[...]


def _outproj_router_call(o2, x2, w, tm):
    n = x2.shape[0]
    row = lambda i: (i, 0)
    c2 = lambda i: (0, 0)
    return pl.pallas_call(
        _outproj_router_kernel,
        grid=(n // tm,),
        in_specs=[
            pl.BlockSpec((tm, D_MODEL), row),
            pl.BlockSpec((tm, D_MODEL), row),
            pl.BlockSpec((D_MODEL, D_MODEL), c2),
            pl.BlockSpec((1, D_MODEL), c2),
            pl.BlockSpec((D_MODEL, N_EXPERTS), c2),
            pl.BlockSpec((1, N_EXPERTS), c2),
        ],
        out_specs=[
            pl.BlockSpec((tm, D_MODEL), row),
            pl.BlockSpec((tm, D_MODEL), row),
            pl.BlockSpec((tm, TOP_K), row),
            pl.BlockSpec((tm, TOP_K), row),
            pl.BlockSpec((tm, TOP_K), row),
            pl.BlockSpec((1, N_EXPERTS), c2),
        ],
        out_shape=[
            jax.ShapeDtypeStruct((n, D_MODEL), F32),
            jax.ShapeDtypeStruct((n, D_MODEL), F32),
            jax.ShapeDtypeStruct((n, TOP_K), jnp.int32),
            jax.ShapeDtypeStruct((n, TOP_K), jnp.int32),
            jax.ShapeDtypeStruct((n, TOP_K), F32),
            jax.ShapeDtypeStruct((1, N_EXPERTS), F32),
        ],
        scratch_shapes=[pltpu.VMEM((1, N_EXPERTS), F32)],
        compiler_params=pltpu.CompilerParams(
            dimension_semantics=("arbitrary",), vmem_limit_bytes=VMEM_LIMIT),
        name="outproj_router",
    )(o2, x2, w["w_o"], w["g_ffn"], w["w_router"], w["b_router"])


MOE_TM = 256


def _row_dma_loop(n_rows, start_row_copies, unroll=8):
    def body(r, c):
        start_row_copies(r)
        return c
    lax.fori_loop(0, n_rows, body, 0, unroll=unroll)


def _dispatch_kernel(pad_start_ref, pad_len_ref, pos_ref, x_ref, xs_ref, zrow, sem, zsem):
    tm = x_ref.shape[0]

    def start(r):
        for k in range(TOP_K):
            p = pos_ref[TOP_K * r + k]
            pltpu.make_async_copy(x_ref.at[pl.ds(r, 1)], xs_ref.at[pl.ds(p, 1)], sem).start(
                priority=k % 2)

    _row_dma_loop(tm, start)
    for k in range(TOP_K):
        pltpu.make_async_copy(x_ref, xs_ref.at[pl.ds(0, tm)], sem).wait()

    @pl.when(pl.program_id(0) == pl.num_programs(0) - 1)
    def _zero_unowned():
        zrow[...] = jnp.zeros(zrow.shape, F32)

        def zero_copy(slot):
            return pltpu.make_async_copy(zrow.at[pl.ds(0, 1)], xs_ref.at[pl.ds(slot, 1)], zsem)

        def per_expert(e, c):
            base = pad_start_ref[e]
            n_pad = pad_len_ref[e]

            def start_row(r, c2):
                zero_copy(base + r).start()
                return c2

            def wait_row(r, c2):
                zero_copy(base + r).wait()
                return c2

            lax.fori_loop(0, n_pad, start_row, 0)
            lax.fori_loop(0, n_pad, wait_row, 0)
            return c

        lax.fori_loop(0, N_EXPERTS, per_expert, 0)


def _dispatch_call(pad_start, pad_len, pos_flat, xn2, n_slots, tm):
    n = xn2.shape[0]
    grid_spec = pltpu.PrefetchScalarGridSpec(
        num_scalar_prefetch=2,
        grid=(n // tm,),
        in_specs=[
            pl.BlockSpec((tm * TOP_K,), lambda i, ps, pn: (i,), memory_space=pltpu.SMEM),
            pl.BlockSpec((tm, D_MODEL), lambda i, ps, pn: (i, 0)),
        ],
        out_specs=pl.BlockSpec(memory_space=pl.ANY),
        scratch_shapes=[pltpu.VMEM((8, D_MODEL), F32), pltpu.SemaphoreType.DMA(()),
                        pltpu.SemaphoreType.DMA(())],
    )
    return pl.pallas_call(
        _dispatch_kernel,
        grid_spec=grid_spec,
        out_shape=jax.ShapeDtypeStruct((n_slots, D_MODEL), F32),
        compiler_params=pltpu.CompilerParams(
            dimension_semantics=("arbitrary",), vmem_limit_bytes=VMEM_LIMIT,
            has_side_effects=True),
        name="moe_dispatch",
    )(pad_start, pad_len, pos_flat, xn2)


def _moe_group_kernel(te_ref, nt_ref, x_ref, wgu_ref, bgu_ref, wd_ref, bd_ref, y_ref):
    del te_ref
    live = pl.program_id(0) < nt_ref[0]

    @pl.when(jnp.logical_not(live))
    def _idle():
        y_ref[...] = jnp.zeros(y_ref.shape, F32)

    @pl.when(live)
    def _tile():
        hgu = jnp.dot(x_ref[...].astype(BF16), wgu_ref[...],
                      preferred_element_type=F32) + bgu_ref[...]
        gate = jnp.minimum(hgu[:, :D_FF], SWIGLU_LIMIT)
        up = jnp.clip(hgu[:, D_FF:], -SWIGLU_LIMIT, SWIGLU_LIMIT)
        act = (up + 1.0) * gate * (1.0 / (1.0 + jnp.exp(-SWIGLU_ALPHA * gate)))
        y_ref[...] = jnp.dot(act.astype(BF16), wd_ref[...],
                             preferred_element_type=F32) + bd_ref[...]


def _moe_group_call(tile_expert, n_tiles, xs, w):
    n_slots = xs.shape[0]
    max_tiles = n_slots // MOE_TM
    live = lambda i, te, nt: jnp.minimum(i, nt[0] - 1)
    wexp = lambda i, te, nt: (te[live(i, te, nt)], 0, 0)
    grid_spec = pltpu.PrefetchScalarGridSpec(
        num_scalar_prefetch=2,
        grid=(max_tiles,),
        in_specs=[
            pl.BlockSpec((MOE_TM, D_MODEL), lambda i, te, nt: (live(i, te, nt), 0)),
            pl.BlockSpec((None, D_MODEL, 2 * D_FF), wexp),
            pl.BlockSpec((None, 1, 2 * D_FF), wexp),
            pl.BlockSpec((None, D_FF, D_MODEL), wexp),
            pl.BlockSpec((None, 1, D_MODEL), wexp),
        ],
        out_specs=pl.BlockSpec((MOE_TM, D_MODEL), lambda i, te, nt: (i, 0)),
    )
    return pl.pallas_call(
        _moe_group_kernel,
        grid_spec=grid_spec,
        out_shape=jax.ShapeDtypeStruct((n_slots, D_MODEL), F32),
        compiler_params=pltpu.CompilerParams(
            dimension_semantics=("arbitrary",), vmem_limit_bytes=VMEM_LIMIT),
        name="moe_group",
    )(tile_expert, n_tiles, xs, w["w_gate_up"], w["b_gate_up"], w["w_down"], w["b_down"])


def _combine_kernel(pos_ref, gt_ref, hp_ref, gfin_ref, ys_ref, y_ref, ybuf, sem, *, final_norm):
    tm = hp_ref.shape[0]

    def start(r):
        for k in range(TOP_K):
            p = pos_ref[TOP_K * r + k]
            pltpu.make_async_copy(ys_ref.at[pl.ds(p, 1)], ybuf.at[k, pl.ds(r, 1)], sem).start(
                priority=k % 2)

    _row_dma_loop(tm, start)
    for k in range(TOP_K):
        pltpu.make_async_copy(ys_ref.at[pl.ds(0, tm)], ybuf.at[k], sem).wait()
    gt = gt_ref[...]
    acc = hp_ref[...]
    for k in range(TOP_K):
        acc = acc + gt[:, k:k + 1] * ybuf[k]
    y_ref[...] = _rms(acc, gfin_ref[...]) if final_norm else acc


def _combine_call(pos_flat, gt, hp, g_final, ys, final_norm, tm):
    n = hp.shape[0]
    return pl.pallas_call(
        functools.partial(_combine_kernel, final_norm=final_norm),
        grid=(n // tm,),
        in_specs=[
            pl.BlockSpec((tm * TOP_K,), lambda i: (i,), memory_space=pltpu.SMEM),
            pl.BlockSpec((tm, TOP_K), lambda i: (i, 0)),
            pl.BlockSpec((tm, D_MODEL), lambda i: (i, 0)),
            pl.BlockSpec((1, D_MODEL), lambda i: (0, 0)),
            pl.BlockSpec(memory_space=pl.ANY),
        ],
        out_specs=pl.BlockSpec((tm, D_MODEL), lambda i: (i, 0)),
        out_shape=jax.ShapeDtypeStruct((n, D_MODEL), F32),
        scratch_shapes=[pltpu.VMEM((TOP_K, tm, D_MODEL), F32), pltpu.SemaphoreType.DMA(())],
        compiler_params=pltpu.CompilerParams(
            dimension_semantics=("arbitrary",), vmem_limit_bytes=VMEM_LIMIT),
        name="moe_combine",
    )(pos_flat, gt, hp, g_final, ys)


def _moe(xn2, ids, rk, gt, cnt, hp, w, g_final, final_norm, tm):
    n = xn2.shape[0]
    n_slots = n * TOP_K + N_EXPERTS * MOE_TM
    counts = cnt[0].astype(jnp.int32)
    tiles_e = (counts + (MOE_TM - 1)) // MOE_TM
    tiles_incl = jnp.cumsum(tiles_e)
    offs = (tiles_incl - tiles_e) * MOE_TM
    n_tiles = tiles_incl[-1:]
    tile_ids = jnp.arange(n_slots // MOE_TM, dtype=jnp.int32)
    tile_expert = jnp.minimum(
        jnp.sum((tile_ids[:, None] >= tiles_incl[None, :]).astype(jnp.int32), axis=1),
        N_EXPERTS - 1)
    onehot = ids[..., None] == jnp.arange(N_EXPERTS, dtype=jnp.int32)
    pos = jnp.sum(jnp.where(onehot, offs, 0), axis=-1) + rk
    pos_flat = pos.reshape(n * TOP_K)
    pad_start = offs + counts
    pad_end = jnp.concatenate([offs[1:], jnp.full((1,), n_slots, jnp.int32)])
    xs = _dispatch_call(pad_start, pad_end - pad_start, pos_flat, xn2, n_slots, tm)
    ys = _moe_group_call(tile_expert, n_tiles, xs, w)
    return _combine_call(pos_flat, gt, hp, g_final, ys, final_norm, tm)


def _rotate_half_cols(wr):
    half = MLA_ROPE // 2
    return jnp.concatenate([-wr[..., half:], wr[..., :half]], axis=-1)


def _prep_layer(l, w_in, g_attn, g_q_a, w_uq, g_kv_a, w_uk, w_uv, lq1, lk1, lq2, lk2, g_subln,
                w_o, g_ffn, w_router, b_router, w_gate_up, b_gate_up, w_down, b_down):
    o_kr = Q_LORA + KV_LORA
    wi = w_in[l]
    k_r = wi[:, o_kr:o_kr + MLA_ROPE]
    w_in_p = jnp.concatenate(
        [wi[:, :o_kr], k_r, _rotate_half_cols(k_r),
         jnp.zeros((D_MODEL, ROPE_PAD - 2 * MLA_ROPE), F32), wi[:, o_kr + MLA_ROPE:]], axis=1)
    wq = w_uq[l].reshape(Q_LORA, MLA_HEADS, MLA_NOPE + MLA_ROPE)
    rope = wq[:, :, MLA_NOPE:]
    w_uq_p = jnp.concatenate(
        [wq[:, :, :MLA_NOPE].reshape(Q_LORA, -1), rope.reshape(Q_LORA, -1),
         _rotate_half_cols(rope).reshape(Q_LORA, -1)], axis=1)
    wk = jnp.transpose(w_uk[l], (1, 2, 0))
    z = jnp.zeros((MLA_NOPE, KV_LORA), F32)
    w_uk_bd = jnp.stack([
        jnp.concatenate([jnp.concatenate([wk[2 * p], z], axis=1),
                         jnp.concatenate([z, wk[2 * p + 1]], axis=1)], axis=0)
        for p in range(MLA_HEADS // 2)])
    wv = jnp.transpose(w_uv[l], (1, 0, 2))
    eye = jnp.eye(MLA_HEADS, dtype=F32)
    w_uv_pad = (wv[:, :, None, :] * eye[:, None, :, None]).reshape(
        MLA_HEADS, KV_LORA, MLA_HEADS * MLA_V)
    w_uv_all = w_uv[l].reshape(KV_LORA, MLA_HEADS * MLA_V)
    return dict(
        g_attn=g_attn[l][None], w_in=w_in_p.astype(BF16), g_q_a=g_q_a[l][None],
        w_uq=w_uq_p.astype(BF16), g_kv_a=g_kv_a[l][None], w_uk=w_uk_bd.astype(BF16),
        w_uv_pad=w_uv_pad.astype(BF16), w_uv_all=w_uv_all.astype(BF16),
        lambda_q1=lq1[l][None], lambda_k1=lk1[l][None], lambda_q2=lq2[l][None],
        lambda_k2=lk2[l][None], g_subln=g_subln[l][None],
        w_o=w_o[l].astype(BF16), g_ffn=g_ffn[l][None], w_router=w_router[l],
        b_router=b_router[l][None], w_gate_up=w_gate_up[l].astype(BF16),
        b_gate_up=b_gate_up[l][:, None, :], w_down=w_down[l].astype(BF16),
        b_down=b_down[l][:, None, :])


def _rope_tables(pos):
    half = MLA_ROPE // 2
    freq = ROPE_THETA ** (-2.0 * jnp.arange(half, dtype=F32) / MLA_ROPE)
    ang = pos[:, None] * freq[None, :]
    cos = jnp.cos(ang)
    sin = jnp.sin(ang)
    cos2 = jnp.concatenate([cos, cos], axis=1)
    sin2 = jnp.concatenate([sin, sin], axis=1)
    tq = jnp.concatenate([jnp.tile(cos2, (1, MLA_HEADS)), jnp.tile(sin2, (1, MLA_HEADS))], axis=1)
    tk = jnp.concatenate(
        [cos2, sin2, jnp.zeros((pos.shape[0], ROPE_PAD - 2 * MLA_ROPE), F32)], axis=1)
    return tq, tk


def _token_tile(n, target):
    t = min(n, target)
    assert n % t == 0
    return t


def kernel(x_prompt, x_sample, cache_ckv, cache_krope, cache_diff_k, cache_diff_v, page_table,
           g_attn, w_in, g_q_a, w_uq, g_kv_a, w_uk, w_uv, lambda_q1, lambda_k1, lambda_q2,
           lambda_k2, g_subln, w_o, g_ffn, w_router, b_router, w_gate_up, b_gate_up, w_down,
           b_down, g_final):
    bp, sp, _ = x_prompt.shape
    bs, ss, _ = x_sample.shape
    depth = w_in.shape[0]
    n_pool, page = cache_ckv.shape[1], cache_ckv.shape[2]
    n_pages = page_table.shape[1]
    past_len = n_pages * page
    n_p = bp * sp
    n_s = bs * ss
    new_pad = LANES

    tq_p, tk_p = _rope_tables(jnp.arange(sp, dtype=F32))
    tq_s, tk_s = _rope_tables(past_len + jnp.arange(ss, dtype=F32))
    tq_s = jnp.tile(tq_s, (bs, 1))
    tk_s = jnp.tile(tk_s, (bs, 1))
    g_fin = g_final[None]

    xp = x_prompt.reshape(n_p, D_MODEL)
    xs = x_sample.reshape(n_s, D_MODEL)
    outs = [[] for _ in range(8)]
    for l in range(depth):
        lambda_init = 0.8 - 0.6 * math.exp(-0.3 * l)
        last = l == depth - 1
        w = _prep_layer(l, w_in, g_attn, g_q_a, w_uq, g_kv_a, w_uk, w_uv, lambda_q1, lambda_k1,
                        lambda_q2, lambda_k2, g_subln, w_o, g_ffn, w_router, b_router,
                        w_gate_up, b_gate_up, w_down, b_down)

        tm_p = _token_tile(sp, 512)
        ckv, kr, dk, dv, qm, dq, kvc, dkb, dvb = _proj_call(xp, tq_p, tk_p, w, tm_p)
        o_p = _prompt_attn_call(qm, dq, kvc, dkb, dvb, w, bp, sp, lambda_init)
        hp, xn2, ids, rk, gt, cnt = _outproj_router_call(o_p, xp, w, tm_p)
        xp = _moe(xn2, ids, rk, gt, cnt, hp, w, g_fin, last, tm_p)
        outs[0].append(ckv.reshape(bp, sp, KV_LORA))
        outs[1].append(kr.reshape(bp, sp, MLA_ROPE))
        outs[2].append(dk.reshape(bp, sp, DIFF_HEADS, 2 * DIFF_QK))
        outs[3].append(dv.reshape(bp, sp, DIFF_HEADS, DIFF_V))

        tm_s = _token_tile(n_s, 512)
        ckv_s, kr_s, dk_s, dv_s, qm_s, dq_s, kvc_s, dkb_s, dvb_s = _proj_call(
            xs, tq_s, tk_s, w, tm_s)
        q_s = qm_s.reshape(MLA_HEADS, bs, ss, KVC).transpose(1, 2, 0, 3).reshape(
            bs, ss * MLA_HEADS, KVC)
        dq_t = jnp.broadcast_to(
            dq_s.astype(F32).reshape(bs, ss, DIFF_HEADS, 1, DIFF_V).transpose(0, 2, 3, 1, 4),
            (bs, DIFF_HEADS, 2, ss, DIFF_V)).reshape(bs, DIFF_HEADS * 2 * ss, DIFF_V)
        padk = lambda a: jnp.pad(a.reshape(bs, ss, a.shape[-1]),
                                 ((0, 0), (0, new_pad - ss), (0, 0)))
        caches = (cache_ckv[l], jnp.swapaxes(cache_krope[l], 1, 2),
                  cache_diff_k[l].reshape(n_pool, page * DIFF_HEADS, DIFF_V),
                  cache_diff_v[l].reshape(n_pool, page * DIFF_HEADS, DIFF_V))
        o_s = _sample_attn_call(page_table, q_s, dq_t, padk(kvc_s), padk(dkb_s), padk(dvb_s),
                                caches, w, lambda_init)
        hs, xn2_s, ids_s, rk_s, gt_s, cnt_s = _outproj_router_call(
            o_s.reshape(n_s, D_MODEL), xs, w, tm_s)
        xs = _moe(xn2_s, ids_s, rk_s, gt_s, cnt_s, hs, w, g_fin, last, tm_s)
        outs[4].append(ckv_s.reshape(bs, ss, KV_LORA))
        outs[5].append(kr_s.reshape(bs, ss, MLA_ROPE))
        outs[6].append(dk_s.reshape(bs, ss, DIFF_HEADS, 2 * DIFF_QK))
        outs[7].append(dv_s.reshape(bs, ss, DIFF_HEADS, DIFF_V))

    return (xp.reshape(bp, sp, D_MODEL), xs.reshape(bs, ss, D_MODEL),
            *[jnp.stack(o) for o in outs])
```

```python
import functools
import math

import jax
import jax.numpy as jnp
from jax import lax
from jax.experimental import pallas as pl
from jax.experimental.pallas import tpu as pltpu

F32 = jnp.float32
BF16 = jnp.bfloat16

D_MODEL = 1024
MLA_HEADS = 8
MLA_NOPE = 64
MLA_ROPE = 32
MLA_V = 64
Q_LORA = 384
KV_LORA = 256
ROPE_THETA = 10000.0
MLA_SCALE = (MLA_NOPE + MLA_ROPE) ** -0.5
DIFF_HEADS = 4
DIFF_QK = 64
DIFF_V = 2 * DIFF_QK
DIFF_SCALE = DIFF_QK ** -0.5
DIFF_COLS = DIFF_HEADS * DIFF_V
N_EXPERTS = 32
TOP_K = 4
D_FF = D_MODEL
SWIGLU_LIMIT = 7.0
SWIGLU_ALPHA = 1.702
EPS = 1e-6

LANES = 128
SUBLANES = 8
ROPE_PAD = LANES
KVC = KV_LORA + ROPE_PAD
IN_PAD_COLS = Q_LORA + KV_LORA + ROPE_PAD + 3 * DIFF_COLS
NEG = -1e30
LOG2E = math.log2(math.e)
VMEM_LIMIT = 56 * 1024 * 1024

_NT = (((1,), (1,)), ((), ()))


def _rms(x, g):
    return x * lax.rsqrt(jnp.mean(x * x, axis=-1, keepdims=True) + EPS) * g


def _alibi_slope(h):
    return 0.25 ** (h + 1) * LOG2E


def _proj_kernel(x_ref, ga_ref, win_ref, gq_ref, wuq_ref, gkv_ref, wuk_ref, tq_ref, tk_ref,
                 ckv_ref, kr_ref, dk_ref, dv_ref, qm_ref, dq_ref, kvc_ref, dkb_ref, dvb_ref):
    tm = x_ref.shape[0]
    xn = _rms(x_ref[...], ga_ref[...])
    h = jnp.dot(xn.astype(BF16), win_ref[...], preferred_element_type=F32)
    o_kv = Q_LORA
    o_kr = o_kv + KV_LORA
    o_dq = o_kr + ROPE_PAD
    o_dk = o_dq + DIFF_COLS
    o_dv = o_dk + DIFF_COLS
    qn = _rms(h[:, :o_kv], gq_ref[...])
    q = jnp.dot(qn.astype(BF16), wuq_ref[...], preferred_element_type=F32)
    n_nope = MLA_HEADS * MLA_NOPE
    n_rope = MLA_HEADS * MLA_ROPE
    tq = tq_ref[...]
    q_rope = (q[:, n_nope:n_nope + n_rope] * tq[:, :n_rope]
              + q[:, n_nope + n_rope:] * tq[:, n_rope:])
    ckv = _rms(h[:, o_kv:o_kr], gkv_ref[...])
    kk = h[:, o_kr:o_dq] * tk_ref[...]
    krope = kk[:, :MLA_ROPE] + kk[:, MLA_ROPE:2 * MLA_ROPE]
    d_k = h[:, o_dk:o_dv]
    d_v = h[:, o_dv:]
    ckv_ref[...] = ckv
    kr_ref[...] = krope
    for hd in range(DIFF_HEADS):
        dk_ref[pl.ds(hd, tm, stride=DIFF_HEADS), :] = d_k[:, DIFF_V * hd:DIFF_V * (hd + 1)]
        dv_ref[pl.ds(hd, tm, stride=DIFF_HEADS), :] = d_v[:, DIFF_V * hd:DIFF_V * (hd + 1)]
    zpad = jnp.zeros((tm, ROPE_PAD - MLA_ROPE), F32)
    kvc_ref[...] = jnp.concatenate([ckv, krope, zpad], axis=-1).astype(BF16)
    dkb_ref[...] = d_k.astype(BF16)
    dvb_ref[...] = d_v.astype(BF16)
    dq_ref[...] = (h[:, o_dq:o_dk] * (DIFF_SCALE * LOG2E)).astype(BF16)
    for p in range(MLA_HEADS // 2):
        ql2 = jnp.dot(q[:, LANES * p:LANES * (p + 1)].astype(BF16), wuk_ref[p],
                      preferred_element_type=F32)
        for u in range(2):
            hh = 2 * p + u
            blk = jnp.concatenate(
                [ql2[:, KV_LORA * u:KV_LORA * (u + 1)],
                 q_rope[:, MLA_ROPE * hh:MLA_ROPE * (hh + 1)], zpad], axis=-1)
            qm_ref[hh] = (blk * (MLA_SCALE * LOG2E)).astype(BF16)


def _proj_call(x2, tq, tk, w, tm):
    n = x2.shape[0]
    ntab = tq.shape[0] // tm
    row = lambda i: (i, 0)
    c2 = lambda i: (0, 0)
    tab = lambda i: (i % ntab, 0)
    in_specs = [
        pl.BlockSpec((tm, D_MODEL), row),
        pl.BlockSpec((1, D_MODEL), c2),
        pl.BlockSpec((D_MODEL, IN_PAD_COLS), c2),
        pl.BlockSpec((1, Q_LORA), c2),
        pl.BlockSpec((Q_LORA, 2 * MLA_HEADS * MLA_NOPE), c2),
        pl.BlockSpec((1, KV_LORA), c2),
        pl.BlockSpec((MLA_HEADS // 2, LANES, 2 * KV_LORA), lambda i: (0, 0, 0)),
        pl.BlockSpec((tm, 2 * MLA_HEADS * MLA_ROPE), tab),
        pl.BlockSpec((tm, ROPE_PAD), tab),
    ]
    out_shape = [
        jax.ShapeDtypeStruct((n, KV_LORA), F32),
        jax.ShapeDtypeStruct((n, MLA_ROPE), F32),
        jax.ShapeDtypeStruct((n * DIFF_HEADS, DIFF_V), F32),
        jax.ShapeDtypeStruct((n * DIFF_HEADS, DIFF_V), F32),
        jax.ShapeDtypeStruct((MLA_HEADS, n, KVC), BF16),
        jax.ShapeDtypeStruct((n, DIFF_COLS), BF16),
        jax.ShapeDtypeStruct((n, KVC), BF16),
        jax.ShapeDtypeStruct((n, DIFF_COLS), BF16),
        jax.ShapeDtypeStruct((n, DIFF_COLS), BF16),
    ]
    out_specs = [
        pl.BlockSpec((tm, KV_LORA), row),
        pl.BlockSpec((tm, MLA_ROPE), row),
        pl.BlockSpec((tm * DIFF_HEADS, DIFF_V), row),
        pl.BlockSpec((tm * DIFF_HEADS, DIFF_V), row),
        pl.BlockSpec((MLA_HEADS, tm, KVC), lambda i: (0, i, 0)),
        pl.BlockSpec((tm, DIFF_COLS), row),
        pl.BlockSpec((tm, KVC), row),
        pl.BlockSpec((tm, DIFF_COLS), row),
        pl.BlockSpec((tm, DIFF_COLS), row),
    ]
    return pl.pallas_call(
        _proj_kernel,
        grid=(n // tm,),
        in_specs=in_specs,
        out_specs=out_specs,
        out_shape=out_shape,
        compiler_params=pltpu.CompilerParams(
            dimension_semantics=("arbitrary",), vmem_limit_bytes=VMEM_LIMIT),
        name="proj",
    )(x2, w["g_attn"], w["w_in"], w["g_q_a"], w["w_uq"], w["g_kv_a"], w["w_uk"], tq, tk)


def _online_update(s, m_ref, l_ref, acc_ref, pv_fn):
    m_prev = m_ref[...]
    m_next = jnp.maximum(m_prev, jnp.max(s, axis=-1, keepdims=True))
    alpha = jnp.exp2(m_prev - m_next)
    wide = lambda a, n: a if n == LANES else jnp.concatenate([a] * (n // LANES), axis=-1)
    p = jnp.exp2(s - wide(m_next, s.shape[-1]))
    l_ref[...] = alpha * l_ref[...] + jnp.sum(p, axis=-1, keepdims=True)
    m_ref[...] = m_next
    acc_ref[...] = acc_ref[...] * wide(alpha, acc_ref.shape[-1]) + pv_fn(p.astype(BF16))


def _lambda_full(lq1_ref, lk1_ref, lq2_ref, lk2_ref, lambda_init):
    a = jnp.sum(lq1_ref[...] * lk1_ref[...], axis=-1, keepdims=True)
    b = jnp.sum(lq2_ref[...] * lk2_ref[...], axis=-1, keepdims=True)
    return jnp.exp(a) - jnp.exp(b) + lambda_init


def _prompt_attn_kernel(qm_ref, dq_ref, kvc_ref, dkb_ref, dvb_ref, wuv_ref, gs_ref,
                        lq1_ref, lk1_ref, lq2_ref, lk2_ref, o_ref,
                        m_a, l_a, acc_a, m_d, l_d, acc_d, *, bq, bk, lambda_init):
    i = pl.program_id(1)
    j = pl.program_id(2)
    last = (i * bq + bq - 1) // bk

    @pl.when(j == 0)
    def _init():
        m_a[...] = jnp.full(m_a.shape, NEG, F32)
        l_a[...] = jnp.zeros(l_a.shape, F32)
        acc_a[...] = jnp.zeros(acc_a.shape, F32)
        m_d[...] = jnp.full(m_d.shape, NEG, F32)
        l_d[...] = jnp.zeros(l_d.shape, F32)
        acc_d[...] = jnp.zeros(acc_d.shape, F32)

    @pl.when(j <= last)
    def _step():
        row = lax.broadcasted_iota(jnp.int32, (bq, bk), 0)
        col = lax.broadcasted_iota(jnp.int32, (bq, bk), 1)
        dist = (i * bq - j * bk) + row - col
        vis = dist >= 0
        distf = dist.astype(F32)

        q = qm_ref[...].reshape(MLA_HEADS * bq, KVC)
        kv = kvc_ref[...]
        s = lax.dot_general(q, kv, _NT, preferred_element_type=F32)
        mask_bias = jnp.where(vis, 0.0, NEG)
        s = (s.reshape(MLA_HEADS, bq, bk) + mask_bias[None]).reshape(MLA_HEADS * bq, bk)
        ckv = kv[:, :KV_LORA]
        _online_update(s, m_a, l_a, acc_a,
                       lambda p: jnp.dot(p, ckv, preferred_element_type=F32))

        lane = lax.broadcasted_iota(jnp.int32, (bq, DIFF_V), 1)
        for h in range(DIFF_HEADS):
            qh = dq_ref[:, DIFF_V * h:DIFF_V * (h + 1)]
            zero = jnp.zeros_like(qh)
            qbd = jnp.concatenate([jnp.where(lane < DIFF_QK, qh, zero),
                                   jnp.where(lane >= DIFF_QK, qh, zero)], axis=0)
            kh = dkb_ref[:, DIFF_V * h:DIFF_V * (h + 1)]
            sd = lax.dot_general(qbd, kh, _NT, preferred_element_type=F32)
            bias = mask_bias - _alibi_slope(h) * distf
            sd = (sd.reshape(2, bq, bk) + bias[None]).reshape(2 * bq, bk)
            vh = dvb_ref[:, DIFF_V * h:DIFF_V * (h + 1)]
            _online_update(sd, m_d.at[h], l_d.at[h], acc_d.at[h],
                           lambda p, vh=vh: jnp.dot(p, vh, preferred_element_type=F32))

    @pl.when(j == last)
    def _finalize():
        o_lat = (acc_a[...] / l_a[:, :1]).astype(BF16).reshape(MLA_HEADS, bq, KV_LORA)
        oa = jnp.dot(o_lat[0], wuv_ref[0], preferred_element_type=F32)
        for h in range(1, MLA_HEADS):
            oa = oa + jnp.dot(o_lat[h], wuv_ref[h], preferred_element_type=F32)
        lam = _lambda_full(lq1_ref, lk1_ref, lq2_ref, lk2_ref, lambda_init)
        outs = [oa]
        for h in range(DIFF_HEADS):
            od = acc_d[h] / l_d[h][:, :1]
            ob = od[:bq] - lam * od[bq:]
            outs.append(_rms(ob, gs_ref[...]) * (1.0 - lambda_init))
        o_ref[...] = jnp.concatenate(outs, axis=-1).astype(o_ref.dtype)


def _prompt_attn_call(qm, dq, kvc, dkb, dvb, w, batch, seq, lambda_init, bq=128, bk=1024):
    bk = min(bk, seq)
    n = batch * seq
    nq = seq // bq
    nk = seq // bk
    kv_idx = lambda b, i, j: (b * nk + jnp.minimum(j, (i * bq + bq - 1) // bk), 0)
    q_idx = lambda b, i, j: (b * nq + i, 0)
    c2 = lambda b, i, j: (0, 0)
    in_specs = [
        pl.BlockSpec((MLA_HEADS, bq, KVC), lambda b, i, j: (0, b * nq + i, 0)),
        pl.BlockSpec((bq, DIFF_COLS), q_idx),
        pl.BlockSpec((bk, KVC), kv_idx),
        pl.BlockSpec((bk, DIFF_COLS), kv_idx),
        pl.BlockSpec((bk, DIFF_COLS), kv_idx),
        pl.BlockSpec((MLA_HEADS, KV_LORA, MLA_HEADS * MLA_V), lambda b, i, j: (0, 0, 0)),
        pl.BlockSpec((1, DIFF_V), c2),
        pl.BlockSpec((1, DIFF_QK), c2),
        pl.BlockSpec((1, DIFF_QK), c2),
        pl.BlockSpec((1, DIFF_QK), c2),
        pl.BlockSpec((1, DIFF_QK), c2),
    ]
    scratch = [
        pltpu.VMEM((MLA_HEADS * bq, LANES), F32),
        pltpu.VMEM((MLA_HEADS * bq, LANES), F32),
        pltpu.VMEM((MLA_HEADS * bq, KV_LORA), F32),
        pltpu.VMEM((DIFF_HEADS, 2 * bq, LANES), F32),
        pltpu.VMEM((DIFF_HEADS, 2 * bq, LANES), F32),
        pltpu.VMEM((DIFF_HEADS, 2 * bq, DIFF_V), F32),
    ]
    return pl.pallas_call(
        functools.partial(_prompt_attn_kernel, bq=bq, bk=bk, lambda_init=lambda_init),
        grid=(batch, nq, nk),
        in_specs=in_specs,
        out_specs=pl.BlockSpec((bq, D_MODEL), q_idx),
        out_shape=jax.ShapeDtypeStruct((n, D_MODEL), BF16),
        scratch_shapes=scratch,
        compiler_params=pltpu.CompilerParams(
            dimension_semantics=("arbitrary", "arbitrary", "arbitrary"),
            vmem_limit_bytes=VMEM_LIMIT),
        name="prompt_attn",
    )(qm, dq, kvc, dkb, dvb, w["w_uv_pad"], w["g_subln"],
      w["lambda_q1"], w["lambda_k1"], w["lambda_q2"], w["lambda_k2"])


def _sample_attn_kernel(pt_ref, q_ref, dqt_ref, kvn_ref, dkn_ref, dvn_ref, wuv_ref, gs_ref,
                        lq1_ref, lk1_ref, lq2_ref, lk2_ref, *rest,
                        spb, pps, page, past_len, dec_seq, lambda_init):
    del pt_ref
    n_pg = 4 * pps * spb
    o_ref = rest[n_pg]
    m_a, l_a, acc_a, m_d, l_d, acc_d = rest[n_pg + 1:]
    step = pl.program_id(1)
    nsteps = pl.num_programs(1)
    rows = dec_seq * MLA_HEADS
    hrows = 2 * dec_seq
    assert rows == DIFF_HEADS * hrows

    r1 = lax.broadcasted_iota(jnp.int32, (rows, 1), 0)
    t_a = (r1 // MLA_HEADS).astype(F32)
    t_d = (r1 % dec_seq).astype(F32)
    h_d = r1 // hrows
    slope = jnp.where(h_d == 0, _alibi_slope(0),
                      jnp.where(h_d == 1, _alibi_slope(1),
                                jnp.where(h_d == 2, _alibi_slope(2), _alibi_slope(3))))
    r_q = lax.broadcasted_iota(jnp.int32, (rows, DIFF_V), 0)
    c_q = lax.broadcasted_iota(jnp.int32, (rows, DIFF_V), 1)
    own = (c_q // DIFF_QK) == ((r_q // dec_seq) % 2)
    kpos = (step * (pps * page)
            + lax.broadcasted_iota(jnp.int32, (rows, pps * page), 1)).astype(F32)
    past_bias = slope * ((past_len + t_d) - kpos)
    zk = jnp.zeros((ROPE_PAD - MLA_ROPE, page), F32)

    def head_rows(a, h):
        return a[hrows * h:hrows * (h + 1)]

    def pair_rows(a, pr):
        return a[2 * hrows * pr:2 * hrows * (pr + 1)]

    def pair_page(ref, pr):
        return jnp.concatenate(
            [ref[pl.ds(2 * pr + u, page, stride=DIFF_HEADS), :].astype(BF16) for u in range(2)],
            axis=-1)

    def seq_refs(g):
        return (m_a.at[g], l_a.at[g], acc_a.at[g], m_d.at[g], l_d.at[g], acc_d.at[g])

    def seq_q(g):
        q = q_ref[g]
        qh = jnp.where(own, dqt_ref[g], 0.0).astype(BF16)
        zq = jnp.zeros((hrows, DIFF_V), BF16)
        qbd = jnp.concatenate(
            [jnp.concatenate([jnp.concatenate([head_rows(qh, 2 * pr), zq], axis=-1),
                              jnp.concatenate([zq, head_rows(qh, 2 * pr + 1)], axis=-1)], axis=0)
             for pr in range(DIFF_HEADS // 2)], axis=0)
        return q, qbd

    @pl.when(step == 0)
    def _new_tokens():
        for g in range(spb):
            ma, la, aa, md, ld, ad = seq_refs(g)
            q, qbd = seq_q(g)
            ma[...] = jnp.full(ma.shape, NEG, F32)
            la[...] = jnp.zeros(la.shape, F32)
            aa[...] = jnp.zeros(aa.shape, F32)
            md[...] = jnp.full(md.shape, NEG, F32)
            ld[...] = jnp.zeros(ld.shape, F32)
            ad[...] = jnp.zeros(ad.shape, F32)
            npad = kvn_ref.shape[1]
            c = lax.broadcasted_iota(jnp.int32, (rows, npad), 1).astype(F32)
            kvn = kvn_ref[g]
            s = lax.dot_general(q, kvn, _NT, preferred_element_type=F32)
            s = jnp.where(c <= t_a, s, NEG)
            _online_update(s, ma, la, aa, lambda p, kvn=kvn: jnp.dot(
                p, kvn[:, :KV_LORA], preferred_element_type=F32))
            sd = jnp.concatenate(
                [lax.dot_general(pair_rows(qbd, pr),
                                 dkn_ref[g, :, 2 * DIFF_V * pr:2 * DIFF_V * (pr + 1)],
                                 _NT, preferred_element_type=F32)
                 for pr in range(DIFF_HEADS // 2)], axis=0)
            sd = sd + jnp.where(c <= t_d, -slope * (t_d - c), NEG)
            _online_update(sd, md, ld, ad, lambda p, g=g: jnp.concatenate(
                [jnp.dot(pair_rows(p, pr), dvn_ref[g, :, 2 * DIFF_V * pr:2 * DIFF_V * (pr + 1)],
                         preferred_element_type=F32) for pr in range(DIFF_HEADS // 2)], axis=0))

    for g in range(spb):
        pg = rest[4 * pps * g:4 * pps * (g + 1)]
        ckv_refs, kr_refs = pg[0:pps], pg[pps:2 * pps]
        dk_refs, dv_refs = pg[2 * pps:3 * pps], pg[3 * pps:4 * pps]
        ma, la, aa, md, ld, ad = seq_refs(g)
        q, qbd = seq_q(g)
        cks, s_parts = [], []
        sd_parts = [[] for _ in range(DIFF_HEADS // 2)]
        for p in range(pps):
            ck = ckv_refs[p][...].astype(BF16)
            krt = jnp.concatenate([kr_refs[p][...], zk], axis=0).astype(BF16)
            cks.append(ck)
            s_parts.append(
                lax.dot_general(q[:, :KV_LORA], ck, _NT, preferred_element_type=F32)
                + jnp.dot(q[:, KV_LORA:], krt, preferred_element_type=F32))
            for pr in range(DIFF_HEADS // 2):
                sd_parts[pr].append(lax.dot_general(pair_rows(qbd, pr), pair_page(dk_refs[p], pr),
                                                    _NT, preferred_element_type=F32))
        s = jnp.concatenate(s_parts, axis=-1)
        sd = jnp.concatenate([jnp.concatenate(parts, axis=-1) for parts in sd_parts], axis=0)
        sd = sd - past_bias

        def pv_a(p, cks=cks):
            acc = jnp.dot(p[:, :page], cks[0], preferred_element_type=F32)
            for u in range(1, pps):
                acc = acc + jnp.dot(p[:, page * u:page * (u + 1)], cks[u],
                                    preferred_element_type=F32)
            return acc

        def pv_d(p, dv_refs=dv_refs):
            outs = []
            for pr in range(DIFF_HEADS // 2):
                ph = pair_rows(p, pr)
                acc = jnp.dot(ph[:, :page], pair_page(dv_refs[0], pr), preferred_element_type=F32)
                for u in range(1, pps):
                    acc = acc + jnp.dot(ph[:, page * u:page * (u + 1)],
                                        pair_page(dv_refs[u], pr), preferred_element_type=F32)
                outs.append(acc)
            return jnp.concatenate(outs, axis=0)

        _online_update(s, ma, la, aa, pv_a)
        _online_update(sd, md, ld, ad, pv_d)

    @pl.when(step == nsteps - 1)
    def _finalize():
        n_v = MLA_HEADS * MLA_V
        r_a = lax.broadcasted_iota(jnp.int32, (rows, n_v), 0)
        c_a = lax.broadcasted_iota(jnp.int32, (rows, n_v), 1)
        lam = _lambda_full(lq1_ref, lk1_ref, lq2_ref, lk2_ref, lambda_init)
        for g in range(spb):
            _, la, aa, _, ld, ad = seq_refs(g)
            o_lat = (aa[...] / la[:, :1]).astype(BF16)
            full = jnp.dot(o_lat, wuv_ref[...], preferred_element_type=F32)
            fa = jnp.where((c_a // MLA_V) == (r_a % MLA_HEADS), full, 0.0)
            oa = jnp.concatenate(
                [jnp.sum(fa[MLA_HEADS * t:MLA_HEADS * (t + 1)], axis=0, keepdims=True)
                 for t in range(dec_seq)], axis=0)
            od = ad[...] / ld[:, :1]
            segs = [oa]
            for h in range(DIFF_HEADS):
                oh = head_rows(od, h)[:, DIFF_V * (h % 2):DIFF_V * (h % 2 + 1)]
                ob = oh[:dec_seq] - lam * oh[dec_seq:]
                segs.append(_rms(ob, gs_ref[...]) * (1.0 - lambda_init))
            o_ref[g] = jnp.concatenate(segs, axis=-1)


def _sample_attn_call(page_table, q_s, dq_t, kv_new, dk_new, dv_new, caches, w, lambda_init,
                      pps=8, spb=2):
    cache_ckv, cache_krt, cache_dk, cache_dv = caches
    nb, n_pages = page_table.shape
    page = cache_ckv.shape[1]
    dec_seq = q_s.shape[1] // MLA_HEADS
    rows = q_s.shape[1]
    npad = kv_new.shape[1]
    pps = min(pps, n_pages)
    assert n_pages % pps == 0 and nb % spb == 0
    nsteps = n_pages // pps
    seq3 = lambda b, j, pt: (b, 0, 0)
    c2 = lambda b, j, pt: (0, 0)
    in_specs = [
        pl.BlockSpec((spb, rows, KVC), seq3),
        pl.BlockSpec((spb, rows, DIFF_V), seq3),
        pl.BlockSpec((spb, npad, KVC), seq3),
        pl.BlockSpec((spb, npad, DIFF_COLS), seq3),
        pl.BlockSpec((spb, npad, DIFF_COLS), seq3),
        pl.BlockSpec((KV_LORA, MLA_HEADS * MLA_V), c2),
        pl.BlockSpec((1, DIFF_V), c2),
        pl.BlockSpec((1, DIFF_QK), c2),
        pl.BlockSpec((1, DIFF_QK), c2),
        pl.BlockSpec((1, DIFF_QK), c2),
        pl.BlockSpec((1, DIFF_QK), c2),
    ]
    args = [q_s, dq_t, kv_new, dk_new, dv_new, w["w_uv_all"], w["g_subln"],
            w["lambda_q1"], w["lambda_k1"], w["lambda_q2"], w["lambda_k2"]]
    for g in range(spb):
        for cache in (cache_ckv, cache_krt, cache_dk, cache_dv):
            for p in range(pps):
                in_specs.append(pl.BlockSpec(
                    (None,) + cache.shape[1:],
                    lambda b, j, pt, g=g, p=p: (pt[b * spb + g, j * pps + p], 0, 0)))
                args.append(cache)
    scratch = [
        pltpu.VMEM((spb, rows, LANES), F32),
        pltpu.VMEM((spb, rows, LANES), F32),
        pltpu.VMEM((spb, rows, KV_LORA), F32),
        pltpu.VMEM((spb, rows, LANES), F32),
        pltpu.VMEM((spb, rows, LANES), F32),
        pltpu.VMEM((spb, rows, 2 * DIFF_V), F32),
    ]
    grid_spec = pltpu.PrefetchScalarGridSpec(
        num_scalar_prefetch=1,
        grid=(nb // spb, nsteps),
        in_specs=in_specs,
        out_specs=pl.BlockSpec((spb, dec_seq, D_MODEL), seq3),
        scratch_shapes=scratch,
    )
    return pl.pallas_call(
        functools.partial(_sample_attn_kernel, spb=spb, pps=pps, page=page,
                          past_len=float(n_pages * page), dec_seq=dec_seq,
                          lambda_init=lambda_init),
        grid_spec=grid_spec,
        out_shape=jax.ShapeDtypeStruct((nb, dec_seq, D_MODEL), F32),
        compiler_params=pltpu.CompilerParams(
            dimension_semantics=("arbitrary", "arbitrary"), vmem_limit_bytes=VMEM_LIMIT),
        name="sample_attn",
    )(page_table, *args)


def _outproj_router_kernel(o_ref, x_ref, wo_ref, gf_ref, wr_ref, br_ref,
                           hp_ref, xn_ref, ids_ref, rk_ref, gt_ref, cnt_ref, carry):
    tm = x_ref.shape[0]

    @pl.when(pl.program_id(0) == 0)
    def _init():
        carry[...] = jnp.zeros(carry.shape, F32)

    hp = x_ref[...] + jnp.dot(o_ref[...].astype(BF16), wo_ref[...], preferred_element_type=F32)
    hp_ref[...] = hp
    xn = _rms(hp, gf_ref[...])
    xn_ref[...] = xn
    x_hi = xn.astype(BF16)
    x_lo = (xn - x_hi.astype(F32)).astype(BF16)
    hh_hl = jnp.dot(x_hi, wr_ref[...], preferred_element_type=F32)
    lh = jnp.dot(x_lo, wr_ref[:, :N_EXPERTS], preferred_element_type=F32)
    logits = hh_hl[:, :N_EXPERTS] + hh_hl[:, N_EXPERTS:] + lh + br_ref[...]
    lane = lax.broadcasted_iota(jnp.int32, (tm, N_EXPERTS), 1).astype(F32)
    work = logits
    sel = jnp.zeros((tm, N_EXPERTS), F32)
    m1 = None
    hits, idxs = [], []
    for k in range(TOP_K):
        mx = jnp.max(work, axis=-1, keepdims=True)
        if k == 0:
            m1 = mx
        idx = jnp.min(jnp.where(work == mx, lane, float(N_EXPERTS)), axis=-1, keepdims=True)
        hit = lane == idx
        hits.append(hit)
        idxs.append(idx)
        sel = jnp.where(hit, 1.0, sel)
        work = jnp.where(hit, -jnp.inf, work)
    e = jnp.where(sel > 0.0, jnp.exp(logits - m1), 0.0)
    gates = e / jnp.sum(e, axis=-1, keepdims=True)
    r_i = lax.broadcasted_iota(jnp.int32, (tm, tm), 0)
    c_i = lax.broadcasted_iota(jnp.int32, (tm, tm), 1)
    before = jnp.where(r_i > c_i, 1.0, 0.0).astype(BF16)
    rank = carry[...] + jnp.dot(before, sel.astype(BF16), preferred_element_type=F32)
    lane_k = lax.broadcasted_iota(jnp.int32, (tm, TOP_K), 1)
    ids = jnp.zeros((tm, TOP_K), F32)
    rk = jnp.zeros((tm, TOP_K), F32)
    gt = jnp.zeros((tm, TOP_K), F32)
    for k in range(TOP_K):
        rk_k = jnp.sum(jnp.where(hits[k], rank, 0.0), axis=-1, keepdims=True)
        gt_k = jnp.sum(jnp.where(hits[k], gates, 0.0), axis=-1, keepdims=True)
        ids = jnp.where(lane_k == k, idxs[k], ids)
        rk = jnp.where(lane_k == k, rk_k, rk)
        gt = jnp.where(lane_k == k, gt_k, gt)
    ids_ref[...] = ids.astype(jnp.int32)
    rk_ref[...] = rk.astype(jnp.int32)
    gt_ref[...] = gt
    carry[...] = carry[...] + jnp.sum(sel, axis=0, keepdims=True)
    cnt_ref[...] = carry[...]


def _outproj_router_call(o2, x2, w, tm):
    n = x2.shape[0]
    row = lambda i: (i, 0)
    c2 = lambda i: (0, 0)
    return pl.pallas_call(
        _outproj_router_kernel,
        grid=(n // tm,),
        in_specs=[
            pl.BlockSpec((tm, D_MODEL), row),
            pl.BlockSpec((tm, D_MODEL), row),
            pl.BlockSpec((D_MODEL, D_MODEL), c2),
            pl.BlockSpec((1, D_MODEL), c2),
            pl.BlockSpec((D_MODEL, 2 * N_EXPERTS), c2),
            pl.BlockSpec((1, N_EXPERTS), c2),
        ],
        out_specs=[
            pl.BlockSpec((tm, D_MODEL), row),
            pl.BlockSpec((tm, D_MODEL), row),
            pl.BlockSpec((tm, TOP_K), row),
            pl.BlockSpec((tm, TOP_K), row),
            pl.BlockSpec((tm, TOP_K), row),
            pl.BlockSpec((1, N_EXPERTS), c2),
        ],
        out_shape=[
            jax.ShapeDtypeStruct((n, D_MODEL), F32),
            jax.ShapeDtypeStruct((n, D_MODEL), F32),
            jax.ShapeDtypeStruct((n, TOP_K), jnp.int32),
            jax.ShapeDtypeStruct((n, TOP_K), jnp.int32),
            jax.ShapeDtypeStruct((n, TOP_K), F32),
            jax.ShapeDtypeStruct((1, N_EXPERTS), F32),
        ],
        scratch_shapes=[pltpu.VMEM((1, N_EXPERTS), F32)],
        compiler_params=pltpu.CompilerParams(
            dimension_semantics=("arbitrary",), vmem_limit_bytes=VMEM_LIMIT),
        name="outproj_router",
    )(o2, x2, w["w_o"], w["g_ffn"], w["w_router"], w["b_router"])


MOE_TM = 256


def _row_dma_loop(n_rows, start_row_copies, unroll=8):
    def body(r, c):
        start_row_copies(r)
        return c
    lax.fori_loop(0, n_rows, body, 0, unroll=unroll)


def _dispatch_kernel(pad_start_ref, pad_len_ref, pos_ref, x_ref, xs_ref, zbuf, sem, zsem):
    tm = x_ref.shape[0]

    def start(r):
        for k in range(TOP_K):
            p = pos_ref[TOP_K * r + k]
            pltpu.make_async_copy(x_ref.at[pl.ds(r, 1)], xs_ref.at[pl.ds(p, 1)], sem).start()

    _row_dma_loop(tm, start)
    for k in range(TOP_K):
        pltpu.make_async_copy(x_ref, xs_ref.at[pl.ds(0, tm)], sem).wait()

    @pl.when(pl.program_id(0) == pl.num_programs(0) - 1)
    def _zero_unowned():
        zbuf[...] = jnp.zeros(zbuf.shape, F32)

        def zero_copy(slot, rows):
            return pltpu.make_async_copy(zbuf.at[pl.ds(0, rows)], xs_ref.at[pl.ds(slot, rows)],
                                         zsem)

        def for_each_chunk(e, act):
            base = pad_start_ref[e]
            n_pad = pad_len_ref[e]
            head = (SUBLANES - (base & (SUBLANES - 1))) & (SUBLANES - 1)
            for r in range(SUBLANES - 1):
                @pl.when(r < head)
                def _(r=r):
                    act(zero_copy(base + r, 1))

            body = base + head
            rest = n_pad - head
            n_full = rest // MOE_TM

            def full(r, c):
                act(zero_copy(pl.multiple_of(body + r * MOE_TM, SUBLANES), MOE_TM))
                return c

            lax.fori_loop(0, n_full, full, 0)
            done = n_full * MOE_TM
            bit = MOE_TM // 2
            while bit >= SUBLANES:
                take = ((rest - done) & bit) != 0

                @pl.when(take)
                def _(done=done, bit=bit):
                    act(zero_copy(pl.multiple_of(body + done, SUBLANES), bit))

                done = done + jnp.where(take, bit, 0)
                bit //= 2

        def start_all(e, c):
            for_each_chunk(e, lambda cp: cp.start())
            return c

        def wait_all(e, c):
            for_each_chunk(e, lambda cp: cp.wait())
            return c

        lax.fori_loop(0, N_EXPERTS, start_all, 0)
        lax.fori_loop(0, N_EXPERTS, wait_all, 0)


def _dispatch_call(pad_start, pad_len, pos_flat, xn2, n_slots, tm):
    n = xn2.shape[0]
    grid_spec = pltpu.PrefetchScalarGridSpec(
        num_scalar_prefetch=2,
        grid=(n // tm,),
        in_specs=[
            pl.BlockSpec((tm * TOP_K,), lambda i, ps, pn: (i,), memory_space=pltpu.SMEM),
            pl.BlockSpec((tm, D_MODEL), lambda i, ps, pn: (i, 0)),
        ],
        out_specs=pl.BlockSpec(memory_space=pl.ANY),
        scratch_shapes=[pltpu.VMEM((MOE_TM, D_MODEL), F32), pltpu.SemaphoreType.DMA(()),
                        pltpu.SemaphoreType.DMA(())],
    )
    return pl.pallas_call(
        _dispatch_kernel,
        grid_spec=grid_spec,
        out_shape=jax.ShapeDtypeStruct((n_slots, D_MODEL), F32),
        compiler_params=pltpu.CompilerParams(
            dimension_semantics=("arbitrary",), vmem_limit_bytes=VMEM_LIMIT,
            has_side_effects=True),
        name="moe_dispatch",
    )(pad_start, pad_len, pos_flat, xn2)


def _moe_group_kernel(te_ref, nt_ref, x_ref, wgu_ref, bgu_ref, wd_ref, bd_ref, y_ref):
    del te_ref
    live = pl.program_id(0) < nt_ref[0]

    @pl.when(jnp.logical_not(live))
    def _idle():
        y_ref[...] = jnp.zeros(y_ref.shape, F32)

    @pl.when(live)
    def _tile():
        hgu = jnp.dot(x_ref[...].astype(BF16), wgu_ref[...],
                      preferred_element_type=F32) + bgu_ref[...]
        gate = jnp.minimum(hgu[:, :D_FF], SWIGLU_LIMIT)
        up = jnp.clip(hgu[:, D_FF:], -SWIGLU_LIMIT, SWIGLU_LIMIT)
        act = (up + 1.0) * gate * (1.0 / (1.0 + jnp.exp(-SWIGLU_ALPHA * gate)))
        y_ref[...] = jnp.dot(act.astype(BF16), wd_ref[...],
                             preferred_element_type=F32) + bd_ref[...]


def _moe_group_call(tile_expert, n_tiles, xs, w):
    n_slots = xs.shape[0]
    max_tiles = n_slots // MOE_TM
    live = lambda i, te, nt: jnp.minimum(i, nt[0] - 1)
    wexp = lambda i, te, nt: (te[live(i, te, nt)], 0, 0)
    grid_spec = pltpu.PrefetchScalarGridSpec(
        num_scalar_prefetch=2,
        grid=(max_tiles,),
        in_specs=[
            pl.BlockSpec((MOE_TM, D_MODEL), lambda i, te, nt: (live(i, te, nt), 0)),
            pl.BlockSpec((None, D_MODEL, 2 * D_FF), wexp),
            pl.BlockSpec((None, 1, 2 * D_FF), wexp),
            pl.BlockSpec((None, D_FF, D_MODEL), wexp),
            pl.BlockSpec((None, 1, D_MODEL), wexp),
        ],
        out_specs=pl.BlockSpec((MOE_TM, D_MODEL), lambda i, te, nt: (i, 0)),
    )
    return pl.pallas_call(
        _moe_group_kernel,
        grid_spec=grid_spec,
        out_shape=jax.ShapeDtypeStruct((n_slots, D_MODEL), F32),
        compiler_params=pltpu.CompilerParams(
            dimension_semantics=("arbitrary",), vmem_limit_bytes=VMEM_LIMIT),
        name="moe_group",
    )(tile_expert, n_tiles, xs, w["w_gate_up"], w["b_gate_up"], w["w_down"], w["b_down"])


def _combine_kernel(pos_ref, gt_ref, hp_ref, gfin_ref, ys_ref, y_ref, ybuf, sem, *, final_norm):
    tm = hp_ref.shape[0]

    def start(r):
        for k in range(TOP_K):
            p = pos_ref[TOP_K * r + k]
            pltpu.make_async_copy(ys_ref.at[pl.ds(p, 1)], ybuf.at[k, pl.ds(r, 1)], sem).start()

    _row_dma_loop(tm, start)
    for k in range(TOP_K):
        pltpu.make_async_copy(ys_ref.at[pl.ds(0, tm)], ybuf.at[k], sem).wait()
    gt = gt_ref[...]
    acc = hp_ref[...]
    for k in range(TOP_K):
        acc = acc + gt[:, k:k + 1] * ybuf[k]
    y_ref[...] = _rms(acc, gfin_ref[...]) if final_norm else acc


def _combine_call(pos_flat, gt, hp, g_final, ys, final_norm, tm):
    n = hp.shape[0]
    return pl.pallas_call(
        functools.partial(_combine_kernel, final_norm=final_norm),
        grid=(n // tm,),
        in_specs=[
            pl.BlockSpec((tm * TOP_K,), lambda i: (i,), memory_space=pltpu.SMEM),
            pl.BlockSpec((tm, TOP_K), lambda i: (i, 0)),
            pl.BlockSpec((tm, D_MODEL), lambda i: (i, 0)),
            pl.BlockSpec((1, D_MODEL), lambda i: (0, 0)),
            pl.BlockSpec(memory_space=pl.ANY),
        ],
        out_specs=pl.BlockSpec((tm, D_MODEL), lambda i: (i, 0)),
        out_shape=jax.ShapeDtypeStruct((n, D_MODEL), F32),
        scratch_shapes=[pltpu.VMEM((TOP_K, tm, D_MODEL), F32), pltpu.SemaphoreType.DMA(())],
        compiler_params=pltpu.CompilerParams(
            dimension_semantics=("arbitrary",), vmem_limit_bytes=VMEM_LIMIT),
        name="moe_combine",
    )(pos_flat, gt, hp, g_final, ys)


def _moe(xn2, ids, rk, gt, cnt, hp, w, g_final, final_norm, tm):
    n = xn2.shape[0]
    n_slots = n * TOP_K + N_EXPERTS * MOE_TM
    counts = cnt[0].astype(jnp.int32)
    tiles_e = (counts + (MOE_TM - 1)) // MOE_TM
    tiles_incl = jnp.cumsum(tiles_e)
    offs = (tiles_incl - tiles_e) * MOE_TM
    n_tiles = tiles_incl[-1:]
    tile_ids = jnp.arange(n_slots // MOE_TM, dtype=jnp.int32)
    tile_expert = jnp.minimum(
        jnp.sum((tile_ids[:, None] >= tiles_incl[None, :]).astype(jnp.int32), axis=1),
        N_EXPERTS - 1)
    onehot = ids[..., None] == jnp.arange(N_EXPERTS, dtype=jnp.int32)
    pos = jnp.sum(jnp.where(onehot, offs, 0), axis=-1) + rk
    pos_flat = pos.reshape(n * TOP_K)
    pad_start = offs + counts
    pad_end = jnp.concatenate([offs[1:], jnp.full((1,), n_slots, jnp.int32)])
    xs = _dispatch_call(pad_start, pad_end - pad_start, pos_flat, xn2, n_slots, tm)
    ys = _moe_group_call(tile_expert, n_tiles, xs, w)
    return _combine_call(pos_flat, gt, hp, g_final, ys, final_norm, tm)


def _hi_lo_cols(w):
    hi = w.astype(BF16)
    lo = (w - hi.astype(F32)).astype(BF16)
    return jnp.concatenate([hi, lo], axis=1)


def _rotate_half_cols(wr):
    half = MLA_ROPE // 2
    return jnp.concatenate([-wr[..., half:], wr[..., :half]], axis=-1)


def _prep_layer(l, w_in, g_attn, g_q_a, w_uq, g_kv_a, w_uk, w_uv, lq1, lk1, lq2, lk2, g_subln,
                w_o, g_ffn, w_router, b_router, w_gate_up, b_gate_up, w_down, b_down):
    o_kr = Q_LORA + KV_LORA
    wi = w_in[l]
    k_r = wi[:, o_kr:o_kr + MLA_ROPE]
    w_in_p = jnp.concatenate(
        [wi[:, :o_kr], k_r, _rotate_half_cols(k_r),
         jnp.zeros((D_MODEL, ROPE_PAD - 2 * MLA_ROPE), F32), wi[:, o_kr + MLA_ROPE:]], axis=1)
    wq = w_uq[l].reshape(Q_LORA, MLA_HEADS, MLA_NOPE + MLA_ROPE)
    rope = wq[:, :, MLA_NOPE:]
    w_uq_p = jnp.concatenate(
        [wq[:, :, :MLA_NOPE].reshape(Q_LORA, -1), rope.reshape(Q_LORA, -1),
         _rotate_half_cols(rope).reshape(Q_LORA, -1)], axis=1)
    wk = jnp.transpose(w_uk[l], (1, 2, 0))
    z = jnp.zeros((MLA_NOPE, KV_LORA), F32)
    w_uk_bd = jnp.stack([
        jnp.concatenate([jnp.concatenate([wk[2 * p], z], axis=1),
                         jnp.concatenate([z, wk[2 * p + 1]], axis=1)], axis=0)
        for p in range(MLA_HEADS // 2)])
    wv = jnp.transpose(w_uv[l], (1, 0, 2))
    eye = jnp.eye(MLA_HEADS, dtype=F32)
    w_uv_pad = (wv[:, :, None, :] * eye[:, None, :, None]).reshape(
        MLA_HEADS, KV_LORA, MLA_HEADS * MLA_V)
    w_uv_all = w_uv[l].reshape(KV_LORA, MLA_HEADS * MLA_V)
    return dict(
        g_attn=g_attn[l][None], w_in=w_in_p.astype(BF16), g_q_a=g_q_a[l][None],
        w_uq=w_uq_p.astype(BF16), g_kv_a=g_kv_a[l][None], w_uk=w_uk_bd.astype(BF16),
        w_uv_pad=w_uv_pad.astype(BF16), w_uv_all=w_uv_all.astype(BF16),
        lambda_q1=lq1[l][None], lambda_k1=lk1[l][None], lambda_q2=lq2[l][None],
        lambda_k2=lk2[l][None], g_subln=g_subln[l][None],
        w_o=w_o[l].astype(BF16), g_ffn=g_ffn[l][None], w_router=_hi_lo_cols(w_router[l]),
        b_router=b_router[l][None], w_gate_up=w_gate_up[l].astype(BF16),
        b_gate_up=b_gate_up[l][:, None, :], w_down=w_down[l].astype(BF16),
        b_down=b_down[l][:, None, :])


def _rope_tables(pos):
    half = MLA_ROPE // 2
    freq = ROPE_THETA ** (-2.0 * jnp.arange(half, dtype=F32) / MLA_ROPE)
    ang = pos[:, None] * freq[None, :]
    cos = jnp.cos(ang)
    sin = jnp.sin(ang)
    cos2 = jnp.concatenate([cos, cos], axis=1)
    sin2 = jnp.concatenate([sin, sin], axis=1)
    tq = jnp.concatenate([jnp.tile(cos2, (1, MLA_HEADS)), jnp.tile(sin2, (1, MLA_HEADS))], axis=1)
    tk = jnp.concatenate(
        [cos2, sin2, jnp.zeros((pos.shape[0], ROPE_PAD - 2 * MLA_ROPE), F32)], axis=1)
    return tq, tk


def _token_tile(n, target):
    t = min(n, target)
    assert n % t == 0
    return t


def kernel(x_prompt, x_sample, cache_ckv, cache_krope, cache_diff_k, cache_diff_v, page_table,
           g_attn, w_in, g_q_a, w_uq, g_kv_a, w_uk, w_uv, lambda_q1, lambda_k1, lambda_q2,
           lambda_k2, g_subln, w_o, g_ffn, w_router, b_router, w_gate_up, b_gate_up, w_down,
           b_down, g_final):
    bp, sp, _ = x_prompt.shape
    bs, ss, _ = x_sample.shape
    depth = w_in.shape[0]
    n_pool, page = cache_ckv.shape[1], cache_ckv.shape[2]
    n_pages = page_table.shape[1]
    past_len = n_pages * page
    n_p = bp * sp
    n_s = bs * ss
    new_pad = LANES

    tq_p, tk_p = _rope_tables(jnp.arange(sp, dtype=F32))
    tq_s, tk_s = _rope_tables(past_len + jnp.arange(ss, dtype=F32))
    tq_s = jnp.tile(tq_s, (bs, 1))
    tk_s = jnp.tile(tk_s, (bs, 1))
    g_fin = g_final[None]

    xp = x_prompt.reshape(n_p, D_MODEL)
    xs = x_sample.reshape(n_s, D_MODEL)
    outs = [[] for _ in range(8)]
    for l in range(depth):
        lambda_init = 0.8 - 0.6 * math.exp(-0.3 * l)
        last = l == depth - 1
        w = _prep_layer(l, w_in, g_attn, g_q_a, w_uq, g_kv_a, w_uk, w_uv, lambda_q1, lambda_k1,
                        lambda_q2, lambda_k2, g_subln, w_o, g_ffn, w_router, b_router,
                        w_gate_up, b_gate_up, w_down, b_down)

        tm_p = _token_tile(sp, 512)
        ckv, kr, dk, dv, qm, dq, kvc, dkb, dvb = _proj_call(xp, tq_p, tk_p, w, tm_p)
        o_p = _prompt_attn_call(qm, dq, kvc, dkb, dvb, w, bp, sp, lambda_init)
        hp, xn2, ids, rk, gt, cnt = _outproj_router_call(o_p, xp, w, tm_p)
        xp = _moe(xn2, ids, rk, gt, cnt, hp, w, g_fin, last, tm_p)
        outs[0].append(ckv.reshape(bp, sp, KV_LORA))
        outs[1].append(kr.reshape(bp, sp, MLA_ROPE))
        outs[2].append(dk.reshape(bp, sp, DIFF_HEADS, 2 * DIFF_QK))
        outs[3].append(dv.reshape(bp, sp, DIFF_HEADS, DIFF_V))

        tm_s = _token_tile(n_s, 512)
        ckv_s, kr_s, dk_s, dv_s, qm_s, dq_s, kvc_s, dkb_s, dvb_s = _proj_call(
            xs, tq_s, tk_s, w, tm_s)
        q_s = qm_s.reshape(MLA_HEADS, bs, ss, KVC).transpose(1, 2, 0, 3).reshape(
            bs, ss * MLA_HEADS, KVC)
        dq_t = jnp.broadcast_to(
            dq_s.astype(F32).reshape(bs, ss, DIFF_HEADS, 1, DIFF_V).transpose(0, 2, 3, 1, 4),
            (bs, DIFF_HEADS, 2, ss, DIFF_V)).reshape(bs, DIFF_HEADS * 2 * ss, DIFF_V)
        padk = lambda a: jnp.pad(a.reshape(bs, ss, a.shape[-1]),
                                 ((0, 0), (0, new_pad - ss), (0, 0)))
        caches = (cache_ckv[l], jnp.swapaxes(cache_krope[l], 1, 2),
                  cache_diff_k[l].reshape(n_pool, page * DIFF_HEADS, DIFF_V),
                  cache_diff_v[l].reshape(n_pool, page * DIFF_HEADS, DIFF_V))
        o_s = _sample_attn_call(page_table, q_s, dq_t, padk(kvc_s), padk(dkb_s), padk(dvb_s),
                                caches, w, lambda_init)
        hs, xn2_s, ids_s, rk_s, gt_s, cnt_s = _outproj_router_call(
            o_s.reshape(n_s, D_MODEL), xs, w, tm_s)
        xs = _moe(xn2_s, ids_s, rk_s, gt_s, cnt_s, hs, w, g_fin, last, tm_s)
        outs[4].append(ckv_s.reshape(bs, ss, KV_LORA))
        outs[5].append(kr_s.reshape(bs, ss, MLA_ROPE))
        outs[6].append(dk_s.reshape(bs, ss, DIFF_HEADS, 2 * DIFF_QK))
        outs[7].append(dv_s.reshape(bs, ss, DIFF_HEADS, DIFF_V))

    return (xp.reshape(bp, sp, D_MODEL), xs.reshape(bs, ss, D_MODEL),
            *[jnp.stack(o) for o in outs])
```

```python
import functools
import math

import jax
import jax.numpy as jnp
from jax import lax
from jax.experimental import pallas as pl
from jax.experimental.pallas import tpu as pltpu

F32 = jnp.float32
BF16 = jnp.bfloat16

D_MODEL = 1024
MLA_HEADS = 8
MLA_NOPE = 64
MLA_ROPE = 32
MLA_V = 64
Q_LORA = 384
KV_LORA = 256
ROPE_THETA = 10000.0
MLA_SCALE = (MLA_NOPE + MLA_ROPE) ** -0.5
DIFF_HEADS = 4
DIFF_QK = 64
DIFF_V = 2 * DIFF_QK
DIFF_SCALE = DIFF_QK ** -0.5
DIFF_COLS = DIFF_HEADS * DIFF_V
N_EXPERTS = 32
TOP_K = 4
D_FF = D_MODEL
SWIGLU_LIMIT = 7.0
SWIGLU_ALPHA = 1.702
EPS = 1e-6

LANES = 128
SUBLANES = 8
ROPE_PAD = LANES
KVC = KV_LORA + ROPE_PAD
IN_PAD_COLS = Q_LORA + KV_LORA + ROPE_PAD + 3 * DIFF_COLS
NEG = -1e30
LOG2E = math.log2(math.e)
VMEM_LIMIT = 56 * 1024 * 1024

_NT = (((1,), (1,)), ((), ()))


def _rms(x, g):
    return x * lax.rsqrt(jnp.mean(x * x, axis=-1, keepdims=True) + EPS) * g


def _alibi_slope(h):
    return 0.25 ** (h + 1) * LOG2E


def _proj_kernel(x_ref, ga_ref, win_ref, gq_ref, wuq_ref, gkv_ref, wuk_ref, tq_ref, tk_ref,
                 ckv_ref, kr_ref, dk_ref, dv_ref, qm_ref, dq_ref, kvc_ref, dkb_ref, dvb_ref):
    tm = x_ref.shape[0]
    xn = _rms(x_ref[...], ga_ref[...])
    h = jnp.dot(xn.astype(BF16), win_ref[...], preferred_element_type=F32)
    o_kv = Q_LORA
    o_kr = o_kv + KV_LORA
    o_dq = o_kr + ROPE_PAD
    o_dk = o_dq + DIFF_COLS
    o_dv = o_dk + DIFF_COLS
    qn = _rms(h[:, :o_kv], gq_ref[...])
    q = jnp.dot(qn.astype(BF16), wuq_ref[...], preferred_element_type=F32)
    n_nope = MLA_HEADS * MLA_NOPE
    n_rope = MLA_HEADS * MLA_ROPE
    tq = tq_ref[...]
    q_rope = (q[:, n_nope:n_nope + n_rope] * tq[:, :n_rope]
              + q[:, n_nope + n_rope:] * tq[:, n_rope:])
    ckv = _rms(h[:, o_kv:o_kr], gkv_ref[...])
    kk = h[:, o_kr:o_dq] * tk_ref[...]
    krope = kk[:, :MLA_ROPE] + kk[:, MLA_ROPE:2 * MLA_ROPE]
    d_k = h[:, o_dk:o_dv]
    d_v = h[:, o_dv:]
    ckv_ref[...] = ckv
    kr_ref[...] = krope
    for hd in range(DIFF_HEADS):
        dk_ref[pl.ds(hd, tm, stride=DIFF_HEADS), :] = d_k[:, DIFF_V * hd:DIFF_V * (hd + 1)]
        dv_ref[pl.ds(hd, tm, stride=DIFF_HEADS), :] = d_v[:, DIFF_V * hd:DIFF_V * (hd + 1)]
    zpad = jnp.zeros((tm, ROPE_PAD - MLA_ROPE), F32)
    kvc_ref[...] = jnp.concatenate([ckv, krope, zpad], axis=-1).astype(BF16)
    dkb_ref[...] = d_k.astype(BF16)
    dvb_ref[...] = d_v.astype(BF16)
    dq_ref[...] = (h[:, o_dq:o_dk] * (DIFF_SCALE * LOG2E)).astype(BF16)
    for p in range(MLA_HEADS // 2):
        ql2 = jnp.dot(q[:, LANES * p:LANES * (p + 1)].astype(BF16), wuk_ref[p],
                      preferred_element_type=F32)
        for u in range(2):
            hh = 2 * p + u
            blk = jnp.concatenate(
                [ql2[:, KV_LORA * u:KV_LORA * (u + 1)],
                 q_rope[:, MLA_ROPE * hh:MLA_ROPE * (hh + 1)], zpad], axis=-1)
            qm_ref[hh] = (blk * (MLA_SCALE * LOG2E)).astype(BF16)


def _proj_call(x2, tq, tk, w, tm):
    n = x2.shape[0]
    ntab = tq.shape[0] // tm
    row = lambda i: (i, 0)
    c2 = lambda i: (0, 0)
    tab = lambda i: (i % ntab, 0)
    in_specs = [
        pl.BlockSpec((tm, D_MODEL), row),
        pl.BlockSpec((1, D_MODEL), c2),
        pl.BlockSpec((D_MODEL, IN_PAD_COLS), c2),
        pl.BlockSpec((1, Q_LORA), c2),
        pl.BlockSpec((Q_LORA, 2 * MLA_HEADS * MLA_NOPE), c2),
        pl.BlockSpec((1, KV_LORA), c2),
        pl.BlockSpec((MLA_HEADS // 2, LANES, 2 * KV_LORA), lambda i: (0, 0, 0)),
        pl.BlockSpec((tm, 2 * MLA_HEADS * MLA_ROPE), tab),
        pl.BlockSpec((tm, ROPE_PAD), tab),
    ]
    out_shape = [
        jax.ShapeDtypeStruct((n, KV_LORA), F32),
        jax.ShapeDtypeStruct((n, MLA_ROPE), F32),
        jax.ShapeDtypeStruct((n * DIFF_HEADS, DIFF_V), F32),
        jax.ShapeDtypeStruct((n * DIFF_HEADS, DIFF_V), F32),
        jax.ShapeDtypeStruct((MLA_HEADS, n, KVC), BF16),
        jax.ShapeDtypeStruct((n, DIFF_COLS), BF16),
        jax.ShapeDtypeStruct((n, KVC), BF16),
        jax.ShapeDtypeStruct((n, DIFF_COLS), BF16),
        jax.ShapeDtypeStruct((n, DIFF_COLS), BF16),
    ]
    out_specs = [
        pl.BlockSpec((tm, KV_LORA), row),
        pl.BlockSpec((tm, MLA_ROPE), row),
        pl.BlockSpec((tm * DIFF_HEADS, DIFF_V), row),
        pl.BlockSpec((tm * DIFF_HEADS, DIFF_V), row),
        pl.BlockSpec((MLA_HEADS, tm, KVC), lambda i: (0, i, 0)),
        pl.BlockSpec((tm, DIFF_COLS), row),
        pl.BlockSpec((tm, KVC), row),
        pl.BlockSpec((tm, DIFF_COLS), row),
        pl.BlockSpec((tm, DIFF_COLS), row),
    ]
    return pl.pallas_call(
        _proj_kernel,
        grid=(n // tm,),
        in_specs=in_specs,
        out_specs=out_specs,
        out_shape=out_shape,
        compiler_params=pltpu.CompilerParams(
            dimension_semantics=("arbitrary",), vmem_limit_bytes=VMEM_LIMIT),
        name="proj",
    )(x2, w["g_attn"], w["w_in"], w["g_q_a"], w["w_uq"], w["g_kv_a"], w["w_uk"], tq, tk)


def _online_update(s, m_ref, l_ref, acc_ref, pv_fn):
    m_prev = m_ref[...]
    m_next = jnp.maximum(m_prev, jnp.max(s, axis=-1, keepdims=True))
    alpha = jnp.exp2(m_prev - m_next)
    wide = lambda a, n: a if n == LANES else jnp.concatenate([a] * (n // LANES), axis=-1)
    p = jnp.exp2(s - wide(m_next, s.shape[-1]))
    l_ref[...] = alpha * l_ref[...] + jnp.sum(p, axis=-1, keepdims=True)
    m_ref[...] = m_next
    acc_ref[...] = acc_ref[...] * wide(alpha, acc_ref.shape[-1]) + pv_fn(p.astype(BF16))


def _lambda_full(lq1_ref, lk1_ref, lq2_ref, lk2_ref, lambda_init):
    a = jnp.sum(lq1_ref[...] * lk1_ref[...], axis=-1, keepdims=True)
    b = jnp.sum(lq2_ref[...] * lk2_ref[...], axis=-1, keepdims=True)
    return jnp.exp(a) - jnp.exp(b) + lambda_init


def _prompt_attn_kernel(qm_ref, dq_ref, kvc_ref, dkb_ref, dvb_ref, wuv_ref, gs_ref,
                        lq1_ref, lk1_ref, lq2_ref, lk2_ref, o_ref,
                        m_a, l_a, acc_a, m_d, l_d, acc_d, *, bq, bk, lambda_init):
    i = pl.program_id(1)
    j = pl.program_id(2)
    last = (i * bq + bq - 1) // bk

    @pl.when(j == 0)
    def _init():
        m_a[...] = jnp.full(m_a.shape, NEG, F32)
        l_a[...] = jnp.zeros(l_a.shape, F32)
        acc_a[...] = jnp.zeros(acc_a.shape, F32)
        m_d[...] = jnp.full(m_d.shape, NEG, F32)
        l_d[...] = jnp.zeros(l_d.shape, F32)
        acc_d[...] = jnp.zeros(acc_d.shape, F32)

    @pl.when(j <= last)
    def _step():
        row = lax.broadcasted_iota(jnp.int32, (bq, bk), 0)
        col = lax.broadcasted_iota(jnp.int32, (bq, bk), 1)
        dist = (i * bq - j * bk) + row - col
        vis = dist >= 0
        distf = dist.astype(F32)

        q = qm_ref[...].reshape(MLA_HEADS * bq, KVC)
        kv = kvc_ref[...]
        s = lax.dot_general(q, kv, _NT, preferred_element_type=F32)
        mask_bias = jnp.where(vis, 0.0, NEG)
        s = (s.reshape(MLA_HEADS, bq, bk) + mask_bias[None]).reshape(MLA_HEADS * bq, bk)
        ckv = kv[:, :KV_LORA]
        _online_update(s, m_a, l_a, acc_a,
                       lambda p: jnp.dot(p, ckv, preferred_element_type=F32))

        lane = lax.broadcasted_iota(jnp.int32, (bq, DIFF_V), 1)
        for h in range(DIFF_HEADS):
            qh = dq_ref[:, DIFF_V * h:DIFF_V * (h + 1)]
            zero = jnp.zeros_like(qh)
            qbd = jnp.concatenate([jnp.where(lane < DIFF_QK, qh, zero),
                                   jnp.where(lane >= DIFF_QK, qh, zero)], axis=0)
            kh = dkb_ref[:, DIFF_V * h:DIFF_V * (h + 1)]
            sd = lax.dot_general(qbd, kh, _NT, preferred_element_type=F32)
            bias = mask_bias - _alibi_slope(h) * distf
            sd = (sd.reshape(2, bq, bk) + bias[None]).reshape(2 * bq, bk)
            vh = dvb_ref[:, DIFF_V * h:DIFF_V * (h + 1)]
            _online_update(sd, m_d.at[h], l_d.at[h], acc_d.at[h],
                           lambda p, vh=vh: jnp.dot(p, vh, preferred_element_type=F32))

    @pl.when(j == last)
    def _finalize():
        inv_a = 1.0 / l_a[...]
        o_lat = (acc_a[...] * jnp.concatenate([inv_a] * (KV_LORA // LANES), axis=-1)
                 ).astype(BF16).reshape(MLA_HEADS, bq, KV_LORA)
        oa = jnp.dot(o_lat[0], wuv_ref[0], preferred_element_type=F32)
        for h in range(1, MLA_HEADS):
            oa = oa + jnp.dot(o_lat[h], wuv_ref[h], preferred_element_type=F32)
        lam = _lambda_full(lq1_ref, lk1_ref, lq2_ref, lk2_ref, lambda_init)
        outs = [oa]
        for h in range(DIFF_HEADS):
            od = acc_d[h] * (1.0 / l_d[h])
            ob = od[:bq] - lam * od[bq:]
            outs.append(_rms(ob, gs_ref[...]) * (1.0 - lambda_init))
        o_ref[...] = jnp.concatenate(outs, axis=-1).astype(o_ref.dtype)


def _prompt_attn_call(qm, dq, kvc, dkb, dvb, w, batch, seq, lambda_init, bq=128, bk=1024):
    bk = min(bk, seq)
    n = batch * seq
    nq = seq // bq
    nk = seq // bk
    kv_idx = lambda b, i, j: (b * nk + jnp.minimum(j, (i * bq + bq - 1) // bk), 0)
    q_idx = lambda b, i, j: (b * nq + i, 0)
    c2 = lambda b, i, j: (0, 0)
    in_specs = [
        pl.BlockSpec((MLA_HEADS, bq, KVC), lambda b, i, j: (0, b * nq + i, 0)),
        pl.BlockSpec((bq, DIFF_COLS), q_idx),
        pl.BlockSpec((bk, KVC), kv_idx),
        pl.BlockSpec((bk, DIFF_COLS), kv_idx),
        pl.BlockSpec((bk, DIFF_COLS), kv_idx),
        pl.BlockSpec((MLA_HEADS, KV_LORA, MLA_HEADS * MLA_V), lambda b, i, j: (0, 0, 0)),
        pl.BlockSpec((1, DIFF_V), c2),
        pl.BlockSpec((1, DIFF_QK), c2),
        pl.BlockSpec((1, DIFF_QK), c2),
        pl.BlockSpec((1, DIFF_QK), c2),
        pl.BlockSpec((1, DIFF_QK), c2),
    ]
    scratch = [
        pltpu.VMEM((MLA_HEADS * bq, LANES), F32),
        pltpu.VMEM((MLA_HEADS * bq, LANES), F32),
        pltpu.VMEM((MLA_HEADS * bq, KV_LORA), F32),
        pltpu.VMEM((DIFF_HEADS, 2 * bq, LANES), F32),
        pltpu.VMEM((DIFF_HEADS, 2 * bq, LANES), F32),
        pltpu.VMEM((DIFF_HEADS, 2 * bq, DIFF_V), F32),
    ]
    return pl.pallas_call(
        functools.partial(_prompt_attn_kernel, bq=bq, bk=bk, lambda_init=lambda_init),
        grid=(batch, nq, nk),
        in_specs=in_specs,
        out_specs=pl.BlockSpec((bq, D_MODEL), q_idx),
        out_shape=jax.ShapeDtypeStruct((n, D_MODEL), BF16),
        scratch_shapes=scratch,
        compiler_params=pltpu.CompilerParams(
            dimension_semantics=("arbitrary", "arbitrary", "arbitrary"),
            vmem_limit_bytes=VMEM_LIMIT),
        name="prompt_attn",
    )(qm, dq, kvc, dkb, dvb, w["w_uv_pad"], w["g_subln"],
      w["lambda_q1"], w["lambda_k1"], w["lambda_q2"], w["lambda_k2"])


def _sample_attn_kernel(pt_ref, q_ref, dqt_ref, kvn_ref, dkn_ref, dvn_ref, wuv_ref, gs_ref,
                        lq1_ref, lk1_ref, lq2_ref, lk2_ref, *rest,
                        spb, pps, page, past_len, dec_seq, lambda_init):
    del pt_ref
    n_pg = 4 * pps * spb
    o_ref = rest[n_pg]
    m_a, l_a, acc_a, m_d, l_d, acc_d = rest[n_pg + 1:]
    step = pl.program_id(1)
    nsteps = pl.num_programs(1)
    rows = dec_seq * MLA_HEADS
    hrows = 2 * dec_seq
    assert rows == DIFF_HEADS * hrows

    r1 = lax.broadcasted_iota(jnp.int32, (rows, 1), 0)
    t_a = (r1 // MLA_HEADS).astype(F32)
    t_d = (r1 % dec_seq).astype(F32)
    h_d = r1 // hrows
    slope = jnp.where(h_d == 0, _alibi_slope(0),
                      jnp.where(h_d == 1, _alibi_slope(1),
                                jnp.where(h_d == 2, _alibi_slope(2), _alibi_slope(3))))
    r_q = lax.broadcasted_iota(jnp.int32, (rows, DIFF_V), 0)
    c_q = lax.broadcasted_iota(jnp.int32, (rows, DIFF_V), 1)
    own = (c_q // DIFF_QK) == ((r_q // dec_seq) % 2)
    kpos = (step * (pps * page)
            + lax.broadcasted_iota(jnp.int32, (rows, pps * page), 1)).astype(F32)
    past_bias = slope * ((past_len + t_d) - kpos)
    zk = jnp.zeros((ROPE_PAD - MLA_ROPE, page), F32)

    def head_rows(a, h):
        return a[hrows * h:hrows * (h + 1)]

    def pair_rows(a, pr):
        return a[2 * hrows * pr:2 * hrows * (pr + 1)]

    def pair_page(ref, pr):
        return jnp.concatenate(
            [ref[pl.ds(2 * pr + u, page, stride=DIFF_HEADS), :].astype(BF16) for u in range(2)],
            axis=-1)

    def seq_refs(g):
        return (m_a.at[g], l_a.at[g], acc_a.at[g], m_d.at[g], l_d.at[g], acc_d.at[g])

    def seq_q(g):
        q = q_ref[g]
        qh = jnp.where(own, dqt_ref[g], 0.0).astype(BF16)
        zq = jnp.zeros((hrows, DIFF_V), BF16)
        qbd = jnp.concatenate(
            [jnp.concatenate([jnp.concatenate([head_rows(qh, 2 * pr), zq], axis=-1),
                              jnp.concatenate([zq, head_rows(qh, 2 * pr + 1)], axis=-1)], axis=0)
             for pr in range(DIFF_HEADS // 2)], axis=0)
        return q, qbd

    @pl.when(step == 0)
    def _new_tokens():
        for g in range(spb):
            ma, la, aa, md, ld, ad = seq_refs(g)
            q, qbd = seq_q(g)
            ma[...] = jnp.full(ma.shape, NEG, F32)
            la[...] = jnp.zeros(la.shape, F32)
            aa[...] = jnp.zeros(aa.shape, F32)
            md[...] = jnp.full(md.shape, NEG, F32)
            ld[...] = jnp.zeros(ld.shape, F32)
            ad[...] = jnp.zeros(ad.shape, F32)
            npad = kvn_ref.shape[1]
            c = lax.broadcasted_iota(jnp.int32, (rows, npad), 1).astype(F32)
            kvn = kvn_ref[g]
            s = lax.dot_general(q, kvn, _NT, preferred_element_type=F32)
            s = jnp.where(c <= t_a, s, NEG)
            _online_update(s, ma, la, aa, lambda p, kvn=kvn: jnp.dot(
                p, kvn[:, :KV_LORA], preferred_element_type=F32))
            sd = jnp.concatenate(
                [lax.dot_general(pair_rows(qbd, pr),
                                 dkn_ref[g, :, 2 * DIFF_V * pr:2 * DIFF_V * (pr + 1)],
                                 _NT, preferred_element_type=F32)
                 for pr in range(DIFF_HEADS // 2)], axis=0)
            sd = sd + jnp.where(c <= t_d, -slope * (t_d - c), NEG)
            _online_update(sd, md, ld, ad, lambda p, g=g: jnp.concatenate(
                [jnp.dot(pair_rows(p, pr), dvn_ref[g, :, 2 * DIFF_V * pr:2 * DIFF_V * (pr + 1)],
                         preferred_element_type=F32) for pr in range(DIFF_HEADS // 2)], axis=0))

    for g in range(spb):
        pg = rest[4 * pps * g:4 * pps * (g + 1)]
        ckv_refs, kr_refs = pg[0:pps], pg[pps:2 * pps]
        dk_refs, dv_refs = pg[2 * pps:3 * pps], pg[3 * pps:4 * pps]
        ma, la, aa, md, ld, ad = seq_refs(g)
        q, qbd = seq_q(g)
        cks, s_parts = [], []
        sd_parts = [[] for _ in range(DIFF_HEADS // 2)]
        for p in range(pps):
            ck = ckv_refs[p][...].astype(BF16)
            krt = jnp.concatenate([kr_refs[p][...], zk], axis=0).astype(BF16)
            cks.append(ck)
            s_parts.append(
                lax.dot_general(q[:, :KV_LORA], ck, _NT, preferred_element_type=F32)
                + jnp.dot(q[:, KV_LORA:], krt, preferred_element_type=F32))
            for pr in range(DIFF_HEADS // 2):
                sd_parts[pr].append(lax.dot_general(pair_rows(qbd, pr), pair_page(dk_refs[p], pr),
                                                    _NT, preferred_element_type=F32))
        s = jnp.concatenate(s_parts, axis=-1)
        sd = jnp.concatenate([jnp.concatenate(parts, axis=-1) for parts in sd_parts], axis=0)
        sd = sd - past_bias

        def pv_a(p, cks=cks):
            acc = jnp.dot(p[:, :page], cks[0], preferred_element_type=F32)
            for u in range(1, pps):
                acc = acc + jnp.dot(p[:, page * u:page * (u + 1)], cks[u],
                                    preferred_element_type=F32)
            return acc

        def pv_d(p, dv_refs=dv_refs):
            outs = []
            for pr in range(DIFF_HEADS // 2):
                ph = pair_rows(p, pr)
                acc = jnp.dot(ph[:, :page], pair_page(dv_refs[0], pr), preferred_element_type=F32)
                for u in range(1, pps):
                    acc = acc + jnp.dot(ph[:, page * u:page * (u + 1)],
                                        pair_page(dv_refs[u], pr), preferred_element_type=F32)
                outs.append(acc)
            return jnp.concatenate(outs, axis=0)

        _online_update(s, ma, la, aa, pv_a)
        _online_update(sd, md, ld, ad, pv_d)

    @pl.when(step == nsteps - 1)
    def _finalize():
        n_v = MLA_HEADS * MLA_V
        r_a = lax.broadcasted_iota(jnp.int32, (rows, n_v), 0)
        c_a = lax.broadcasted_iota(jnp.int32, (rows, n_v), 1)
        lam = _lambda_full(lq1_ref, lk1_ref, lq2_ref, lk2_ref, lambda_init)
        for g in range(spb):
            _, la, aa, _, ld, ad = seq_refs(g)
            o_lat = (aa[...] / la[:, :1]).astype(BF16)
            full = jnp.dot(o_lat, wuv_ref[...], preferred_element_type=F32)
            fa = jnp.where((c_a // MLA_V) == (r_a % MLA_HEADS), full, 0.0)
            oa = jnp.concatenate(
                [jnp.sum(fa[MLA_HEADS * t:MLA_HEADS * (t + 1)], axis=0, keepdims=True)
                 for t in range(dec_seq)], axis=0)
            od = ad[...] / ld[:, :1]
            segs = [oa]
            for h in range(DIFF_HEADS):
                oh = head_rows(od, h)[:, DIFF_V * (h % 2):DIFF_V * (h % 2 + 1)]
                ob = oh[:dec_seq] - lam * oh[dec_seq:]
                segs.append(_rms(ob, gs_ref[...]) * (1.0 - lambda_init))
            o_ref[g] = jnp.concatenate(segs, axis=-1)


def _sample_attn_call(page_table, q_s, dq_t, kv_new, dk_new, dv_new, caches, w, lambda_init,
                      pps=8, spb=2):
    cache_ckv, cache_krt, cache_dk, cache_dv = caches
    nb, n_pages = page_table.shape
    page = cache_ckv.shape[1]
    dec_seq = q_s.shape[1] // MLA_HEADS
    rows = q_s.shape[1]
    npad = kv_new.shape[1]
    pps = min(pps, n_pages)
    assert n_pages % pps == 0 and nb % spb == 0
    nsteps = n_pages // pps
    seq3 = lambda b, j, pt: (b, 0, 0)
    c2 = lambda b, j, pt: (0, 0)
    in_specs = [
        pl.BlockSpec((spb, rows, KVC), seq3),
        pl.BlockSpec((spb, rows, DIFF_V), seq3),
        pl.BlockSpec((spb, npad, KVC), seq3),
        pl.BlockSpec((spb, npad, DIFF_COLS), seq3),
        pl.BlockSpec((spb, npad, DIFF_COLS), seq3),
        pl.BlockSpec((KV_LORA, MLA_HEADS * MLA_V), c2),
        pl.BlockSpec((1, DIFF_V), c2),
        pl.BlockSpec((1, DIFF_QK), c2),
        pl.BlockSpec((1, DIFF_QK), c2),
        pl.BlockSpec((1, DIFF_QK), c2),
        pl.BlockSpec((1, DIFF_QK), c2),
    ]
    args = [q_s, dq_t, kv_new, dk_new, dv_new, w["w_uv_all"], w["g_subln"],
            w["lambda_q1"], w["lambda_k1"], w["lambda_q2"], w["lambda_k2"]]
    for g in range(spb):
        for cache in (cache_ckv, cache_krt, cache_dk, cache_dv):
            for p in range(pps):
                in_specs.append(pl.BlockSpec(
                    (None,) + cache.shape[1:],
                    lambda b, j, pt, g=g, p=p: (pt[b * spb + g, j * pps + p], 0, 0)))
                args.append(cache)
    scratch = [
        pltpu.VMEM((spb, rows, LANES), F32),
        pltpu.VMEM((spb, rows, LANES), F32),
        pltpu.VMEM((spb, rows, KV_LORA), F32),
        pltpu.VMEM((spb, rows, LANES), F32),
        pltpu.VMEM((spb, rows, LANES), F32),
        pltpu.VMEM((spb, rows, 2 * DIFF_V), F32),
    ]
    grid_spec = pltpu.PrefetchScalarGridSpec(
        num_scalar_prefetch=1,
        grid=(nb // spb, nsteps),
        in_specs=in_specs,
        out_specs=pl.BlockSpec((spb, dec_seq, D_MODEL), seq3),
        scratch_shapes=scratch,
    )
    return pl.pallas_call(
        functools.partial(_sample_attn_kernel, spb=spb, pps=pps, page=page,
                          past_len=float(n_pages * page), dec_seq=dec_seq,
                          lambda_init=lambda_init),
        grid_spec=grid_spec,
        out_shape=jax.ShapeDtypeStruct((nb, dec_seq, D_MODEL), F32),
        compiler_params=pltpu.CompilerParams(
            dimension_semantics=("arbitrary", "arbitrary"), vmem_limit_bytes=VMEM_LIMIT),
        name="sample_attn",
    )(page_table, *args)


def _outproj_router_kernel(o_ref, x_ref, wo_ref, gf_ref, wr_ref, br_ref,
                           hp_ref, xn_ref, ids_ref, rk_ref, gt_ref, cnt_ref, carry):
    tm = x_ref.shape[0]

    @pl.when(pl.program_id(0) == 0)
    def _init():
        carry[...] = jnp.zeros(carry.shape, F32)

    hp = x_ref[...] + jnp.dot(o_ref[...].astype(BF16), wo_ref[...], preferred_element_type=F32)
    hp_ref[...] = hp
    xn = _rms(hp, gf_ref[...])
    xn_ref[...] = xn
    x_hi = xn.astype(BF16)
    x_lo = (xn - x_hi.astype(F32)).astype(BF16)
    hh_hl = jnp.dot(x_hi, wr_ref[...], preferred_element_type=F32)
    lh = jnp.dot(x_lo, wr_ref[:, :N_EXPERTS], preferred_element_type=F32)
    logits = hh_hl[:, :N_EXPERTS] + hh_hl[:, N_EXPERTS:] + lh + br_ref[...]
    lane = lax.broadcasted_iota(jnp.int32, (tm, N_EXPERTS), 1).astype(F32)
    work = logits
    sel = jnp.zeros((tm, N_EXPERTS), F32)
    m1 = None
    hits, idxs = [], []
    for k in range(TOP_K):
        mx = jnp.max(work, axis=-1, keepdims=True)
        if k == 0:
            m1 = mx
        idx = jnp.min(jnp.where(work == mx, lane, float(N_EXPERTS)), axis=-1, keepdims=True)
        hit = lane == idx
        hits.append(hit)
        idxs.append(idx)
        sel = jnp.where(hit, 1.0, sel)
        work = jnp.where(hit, -jnp.inf, work)
    e = jnp.where(sel > 0.0, jnp.exp(logits - m1), 0.0)
    gates = e / jnp.sum(e, axis=-1, keepdims=True)
    r_i = lax.broadcasted_iota(jnp.int32, (tm, tm), 0)
    c_i = lax.broadcasted_iota(jnp.int32, (tm, tm), 1)
    before = jnp.where(r_i > c_i, 1.0, 0.0).astype(BF16)
    rank = carry[...] + jnp.dot(before, sel.astype(BF16), preferred_element_type=F32)
    lane_k = lax.broadcasted_iota(jnp.int32, (tm, TOP_K), 1)
    ids = jnp.zeros((tm, TOP_K), F32)
    rk = jnp.zeros((tm, TOP_K), F32)
    gt = jnp.zeros((tm, TOP_K), F32)
    for k in range(TOP_K):
        rk_k = jnp.sum(jnp.where(hits[k], rank, 0.0), axis=-1, keepdims=True)
        gt_k = jnp.sum(jnp.where(hits[k], gates, 0.0), axis=-1, keepdims=True)
        ids = jnp.where(lane_k == k, idxs[k], ids)
        rk = jnp.where(lane_k == k, rk_k, rk)
        gt = jnp.where(lane_k == k, gt_k, gt)
    ids_ref[...] = ids.astype(jnp.int32)
    rk_ref[...] = rk.astype(jnp.int32)
    gt_ref[...] = gt
    carry[...] = carry[...] + jnp.sum(sel, axis=0, keepdims=True)
    cnt_ref[...] = carry[...]


def _outproj_router_call(o2, x2, w, tm):
    n = x2.shape[0]
    row = lambda i: (i, 0)
    c2 = lambda i: (0, 0)
    return pl.pallas_call(
        _outproj_router_kernel,
        grid=(n // tm,),
        in_specs=[
            pl.BlockSpec((tm, D_MODEL), row),
            pl.BlockSpec((tm, D_MODEL), row),
            pl.BlockSpec((D_MODEL, D_MODEL), c2),
            pl.BlockSpec((1, D_MODEL), c2),
            pl.BlockSpec((D_MODEL, 2 * N_EXPERTS), c2),
            pl.BlockSpec((1, N_EXPERTS), c2),
        ],
        out_specs=[
            pl.BlockSpec((tm, D_MODEL), row),
            pl.BlockSpec((tm, D_MODEL), row),
            pl.BlockSpec((tm, TOP_K), row),
            pl.BlockSpec((tm, TOP_K), row),
            pl.BlockSpec((tm, TOP_K), row),
            pl.BlockSpec((1, N_EXPERTS), c2),
        ],
        out_shape=[
            jax.ShapeDtypeStruct((n, D_MODEL), F32),
            jax.ShapeDtypeStruct((n, D_MODEL), F32),
            jax.ShapeDtypeStruct((n, TOP_K), jnp.int32),
            jax.ShapeDtypeStruct((n, TOP_K), jnp.int32),
            jax.ShapeDtypeStruct((n, TOP_K), F32),
            jax.ShapeDtypeStruct((1, N_EXPERTS), F32),
        ],
        scratch_shapes=[pltpu.VMEM((1, N_EXPERTS), F32)],
        compiler_params=pltpu.CompilerParams(
            dimension_semantics=("arbitrary",), vmem_limit_bytes=VMEM_LIMIT),
        name="outproj_router",
    )(o2, x2, w["w_o"], w["g_ffn"], w["w_router"], w["b_router"])


MOE_TM = 256


def _row_dma_loop(n_rows, start_row_copies, unroll=8):
    def body(r, c):
        start_row_copies(r)
        return c
    lax.fori_loop(0, n_rows, body, 0, unroll=unroll)


def _dispatch_kernel(pad_start_ref, pad_len_ref, pos_ref, x_ref, xs_ref, zbuf, sem, zsem):
    tm = x_ref.shape[0]

    def start(r):
        for k in range(TOP_K):
            p = pos_ref[TOP_K * r + k]
            pltpu.make_async_copy(x_ref.at[pl.ds(r, 1)], xs_ref.at[pl.ds(p, 1)], sem).start()

    _row_dma_loop(tm, start)
    for k in range(TOP_K):
        pltpu.make_async_copy(x_ref, xs_ref.at[pl.ds(0, tm)], sem).wait()

    @pl.when(pl.program_id(0) == pl.num_programs(0) - 1)
    def _zero_unowned():
        zbuf[...] = jnp.zeros(zbuf.shape, F32)

        def zero_copy(slot, rows):
            return pltpu.make_async_copy(zbuf.at[pl.ds(0, rows)], xs_ref.at[pl.ds(slot, rows)],
                                         zsem)

        def for_each_chunk(e, act):
            base = pad_start_ref[e]
            n_pad = pad_len_ref[e]
            head = (SUBLANES - (base & (SUBLANES - 1))) & (SUBLANES - 1)
            for r in range(SUBLANES - 1):
                @pl.when(r < head)
                def _(r=r):
                    act(zero_copy(base + r, 1))

            body = base + head
            rest = n_pad - head
            n_full = rest // MOE_TM

            def full(r, c):
                act(zero_copy(pl.multiple_of(body + r * MOE_TM, SUBLANES), MOE_TM))
                return c

            lax.fori_loop(0, n_full, full, 0)
            done = n_full * MOE_TM
            bit = MOE_TM // 2
            while bit >= SUBLANES:
                take = ((rest - done) & bit) != 0

                @pl.when(take)
                def _(done=done, bit=bit):
                    act(zero_copy(pl.multiple_of(body + done, SUBLANES), bit))

                done = done + jnp.where(take, bit, 0)
                bit //= 2

        def start_all(e, c):
            for_each_chunk(e, lambda cp: cp.start())
            return c

        def wait_all(e, c):
            for_each_chunk(e, lambda cp: cp.wait())
            return c

        lax.fori_loop(0, N_EXPERTS, start_all, 0)
        lax.fori_loop(0, N_EXPERTS, wait_all, 0)


def _dispatch_call(pad_start, pad_len, pos_flat, xn2, n_slots, tm):
    n = xn2.shape[0]
    grid_spec = pltpu.PrefetchScalarGridSpec(
        num_scalar_prefetch=2,
        grid=(n // tm,),
        in_specs=[
            pl.BlockSpec((tm * TOP_K,), lambda i, ps, pn: (i,), memory_space=pltpu.SMEM),
            pl.BlockSpec((tm, D_MODEL), lambda i, ps, pn: (i, 0)),
        ],
        out_specs=pl.BlockSpec(memory_space=pl.ANY),
        scratch_shapes=[pltpu.VMEM((MOE_TM, D_MODEL), F32), pltpu.SemaphoreType.DMA(()),
                        pltpu.SemaphoreType.DMA(())],
    )
    return pl.pallas_call(
        _dispatch_kernel,
        grid_spec=grid_spec,
        out_shape=jax.ShapeDtypeStruct((n_slots, D_MODEL), F32),
        compiler_params=pltpu.CompilerParams(
            dimension_semantics=("arbitrary",), vmem_limit_bytes=VMEM_LIMIT,
            has_side_effects=True),
        name="moe_dispatch",
    )(pad_start, pad_len, pos_flat, xn2)


def _moe_group_kernel(te_ref, nt_ref, x_ref, wgu_ref, bgu_ref, wd_ref, bd_ref, y_ref):
    del te_ref
    live = pl.program_id(0) < nt_ref[0]

    @pl.when(jnp.logical_not(live))
    def _idle():
        y_ref[...] = jnp.zeros(y_ref.shape, F32)

    @pl.when(live)
    def _tile():
        hgu = jnp.dot(x_ref[...].astype(BF16), wgu_ref[...].astype(BF16),
                      preferred_element_type=F32) + bgu_ref[...]
        gate = jnp.minimum(hgu[:, :D_FF], SWIGLU_LIMIT)
        up = jnp.clip(hgu[:, D_FF:], -SWIGLU_LIMIT, SWIGLU_LIMIT)
        act = (up + 1.0) * gate * (1.0 / (1.0 + jnp.exp(-SWIGLU_ALPHA * gate)))
        y_ref[...] = jnp.dot(act.astype(BF16), wd_ref[...].astype(BF16),
                             preferred_element_type=F32) + bd_ref[...]


def _moe_group_call(tile_expert, n_tiles, xs, w):
    n_slots = xs.shape[0]
    max_tiles = n_slots // MOE_TM
    live = lambda i, te, nt: jnp.minimum(i, nt[0] - 1)
    wexp = lambda i, te, nt: (te[live(i, te, nt)], 0, 0)
    grid_spec = pltpu.PrefetchScalarGridSpec(
        num_scalar_prefetch=2,
        grid=(max_tiles,),
        in_specs=[
            pl.BlockSpec((MOE_TM, D_MODEL), lambda i, te, nt: (live(i, te, nt), 0)),
            pl.BlockSpec((None, D_MODEL, 2 * D_FF), wexp),
            pl.BlockSpec((None, 1, 2 * D_FF), wexp),
            pl.BlockSpec((None, D_FF, D_MODEL), wexp),
            pl.BlockSpec((None, 1, D_MODEL), wexp),
        ],
        out_specs=pl.BlockSpec((MOE_TM, D_MODEL), lambda i, te, nt: (i, 0)),
    )
    return pl.pallas_call(
        _moe_group_kernel,
        grid_spec=grid_spec,
        out_shape=jax.ShapeDtypeStruct((n_slots, D_MODEL), F32),
        compiler_params=pltpu.CompilerParams(
            dimension_semantics=("arbitrary",), vmem_limit_bytes=VMEM_LIMIT),
        name="moe_group",
    )(tile_expert, n_tiles, xs, w["w_gate_up"], w["b_gate_up"], w["w_down"], w["b_down"])


def _combine_kernel(pos_ref, gt_ref, hp_ref, gfin_ref, ys_ref, y_ref, ybuf, sem, *, final_norm):
    tm = hp_ref.shape[0]

    def start(r):
        for k in range(TOP_K):
            p = pos_ref[TOP_K * r + k]
            pltpu.make_async_copy(ys_ref.at[pl.ds(p, 1)], ybuf.at[k, pl.ds(r, 1)], sem).start()

    _row_dma_loop(tm, start)
    for k in range(TOP_K):
        pltpu.make_async_copy(ys_ref.at[pl.ds(0, tm)], ybuf.at[k], sem).wait()
    gt = gt_ref[...]
    acc = hp_ref[...]
    for k in range(TOP_K):
        acc = acc + gt[:, k:k + 1] * ybuf[k]
    y_ref[...] = _rms(acc, gfin_ref[...]) if final_norm else acc


def _combine_call(pos_flat, gt, hp, g_final, ys, final_norm, tm):
    n = hp.shape[0]
    return pl.pallas_call(
        functools.partial(_combine_kernel, final_norm=final_norm),
        grid=(n // tm,),
        in_specs=[
            pl.BlockSpec((tm * TOP_K,), lambda i: (i,), memory_space=pltpu.SMEM),
            pl.BlockSpec((tm, TOP_K), lambda i: (i, 0)),
            pl.BlockSpec((tm, D_MODEL), lambda i: (i, 0)),
            pl.BlockSpec((1, D_MODEL), lambda i: (0, 0)),
            pl.BlockSpec(memory_space=pl.ANY),
        ],
        out_specs=pl.BlockSpec((tm, D_MODEL), lambda i: (i, 0)),
        out_shape=jax.ShapeDtypeStruct((n, D_MODEL), F32),
        scratch_shapes=[pltpu.VMEM((TOP_K, tm, D_MODEL), F32), pltpu.SemaphoreType.DMA(())],
        compiler_params=pltpu.CompilerParams(
            dimension_semantics=("arbitrary",), vmem_limit_bytes=VMEM_LIMIT),
        name="moe_combine",
    )(pos_flat, gt, hp, g_final, ys)


def _moe(xn2, ids, rk, gt, cnt, hp, w, g_final, final_norm, tm):
    n = xn2.shape[0]
    n_slots = n * TOP_K + N_EXPERTS * MOE_TM
    counts = cnt[0].astype(jnp.int32)
    tiles_e = (counts + (MOE_TM - 1)) // MOE_TM
    tiles_incl = jnp.cumsum(tiles_e)
    offs = (tiles_incl - tiles_e) * MOE_TM
    n_tiles = tiles_incl[-1:]
    tile_ids = jnp.arange(n_slots // MOE_TM, dtype=jnp.int32)
    tile_expert = jnp.minimum(
        jnp.sum((tile_ids[:, None] >= tiles_incl[None, :]).astype(jnp.int32), axis=1),
        N_EXPERTS - 1)
    onehot = ids[..., None] == jnp.arange(N_EXPERTS, dtype=jnp.int32)
    pos = jnp.sum(jnp.where(onehot, offs, 0), axis=-1) + rk
    pos_flat = pos.reshape(n * TOP_K)
    pad_start = offs + counts
    pad_end = jnp.concatenate([offs[1:], jnp.full((1,), n_slots, jnp.int32)])
    xs = _dispatch_call(pad_start, pad_end - pad_start, pos_flat, xn2, n_slots, tm)
    ys = _moe_group_call(tile_expert, n_tiles, xs, w)
    return _combine_call(pos_flat, gt, hp, g_final, ys, final_norm, tm)


def _hi_lo_cols(w):
    hi = w.astype(BF16)
    lo = (w - hi.astype(F32)).astype(BF16)
    return jnp.concatenate([hi, lo], axis=1)


def _rotate_half_cols(wr):
    half = MLA_ROPE // 2
    return jnp.concatenate([-wr[..., half:], wr[..., :half]], axis=-1)


def _prep_layer(l, w_in, g_attn, g_q_a, w_uq, g_kv_a, w_uk, w_uv, lq1, lk1, lq2, lk2, g_subln,
                w_o, g_ffn, w_router, b_router, w_gate_up, b_gate_up, w_down, b_down):
    o_kr = Q_LORA + KV_LORA
    wi = w_in[l]
    k_r = wi[:, o_kr:o_kr + MLA_ROPE]
    w_in_p = jnp.concatenate(
        [wi[:, :o_kr], k_r, _rotate_half_cols(k_r),
         jnp.zeros((D_MODEL, ROPE_PAD - 2 * MLA_ROPE), F32), wi[:, o_kr + MLA_ROPE:]], axis=1)
    wq = w_uq[l].reshape(Q_LORA, MLA_HEADS, MLA_NOPE + MLA_ROPE)
    rope = wq[:, :, MLA_NOPE:]
    w_uq_p = jnp.concatenate(
        [wq[:, :, :MLA_NOPE].reshape(Q_LORA, -1), rope.reshape(Q_LORA, -1),
         _rotate_half_cols(rope).reshape(Q_LORA, -1)], axis=1)
    wk = jnp.transpose(w_uk[l], (1, 2, 0))
    z = jnp.zeros((MLA_NOPE, KV_LORA), F32)
    w_uk_bd = jnp.stack([
        jnp.concatenate([jnp.concatenate([wk[2 * p], z], axis=1),
                         jnp.concatenate([z, wk[2 * p + 1]], axis=1)], axis=0)
        for p in range(MLA_HEADS // 2)])
    wv = jnp.transpose(w_uv[l], (1, 0, 2))
    eye = jnp.eye(MLA_HEADS, dtype=F32)
    w_uv_pad = (wv[:, :, None, :] * eye[:, None, :, None]).reshape(
        MLA_HEADS, KV_LORA, MLA_HEADS * MLA_V)
    w_uv_all = w_uv[l].reshape(KV_LORA, MLA_HEADS * MLA_V)
    return dict(
        g_attn=g_attn[l][None], w_in=w_in_p.astype(BF16), g_q_a=g_q_a[l][None],
        w_uq=w_uq_p.astype(BF16), g_kv_a=g_kv_a[l][None], w_uk=w_uk_bd.astype(BF16),
        w_uv_pad=w_uv_pad.astype(BF16), w_uv_all=w_uv_all.astype(BF16),
        lambda_q1=lq1[l][None], lambda_k1=lk1[l][None], lambda_q2=lq2[l][None],
        lambda_k2=lk2[l][None], g_subln=g_subln[l][None],
        w_o=w_o[l].astype(BF16), g_ffn=g_ffn[l][None], w_router=_hi_lo_cols(w_router[l]),
        b_router=b_router[l][None], w_gate_up=w_gate_up[l],
        b_gate_up=b_gate_up[l][:, None, :], w_down=w_down[l],
        b_down=b_down[l][:, None, :])


def _rope_tables(pos):
    half = MLA_ROPE // 2
    freq = ROPE_THETA ** (-2.0 * jnp.arange(half, dtype=F32) / MLA_ROPE)
    ang = pos[:, None] * freq[None, :]
    cos = jnp.cos(ang)
    sin = jnp.sin(ang)
    cos2 = jnp.concatenate([cos, cos], axis=1)
    sin2 = jnp.concatenate([sin, sin], axis=1)
    tq = jnp.concatenate([jnp.tile(cos2, (1, MLA_HEADS)), jnp.tile(sin2, (1, MLA_HEADS))], axis=1)
    tk = jnp.concatenate(
        [cos2, sin2, jnp.zeros((pos.shape[0], ROPE_PAD - 2 * MLA_ROPE), F32)], axis=1)
    return tq, tk


def _token_tile(n, target):
    t = min(n, target)
    assert n % t == 0
    return t


def kernel(x_prompt, x_sample, cache_ckv, cache_krope, cache_diff_k, cache_diff_v, page_table,
           g_attn, w_in, g_q_a, w_uq, g_kv_a, w_uk, w_uv, lambda_q1, lambda_k1, lambda_q2,
           lambda_k2, g_subln, w_o, g_ffn, w_router, b_router, w_gate_up, b_gate_up, w_down,
           b_down, g_final):
    bp, sp, _ = x_prompt.shape
    bs, ss, _ = x_sample.shape
    depth = w_in.shape[0]
    n_pool, page = cache_ckv.shape[1], cache_ckv.shape[2]
    n_pages = page_table.shape[1]
    past_len = n_pages * page
    n_p = bp * sp
    n_s = bs * ss
    new_pad = LANES

    tq_p, tk_p = _rope_tables(jnp.arange(sp, dtype=F32))
    tq_s, tk_s = _rope_tables(past_len + jnp.arange(ss, dtype=F32))
    tq_s = jnp.tile(tq_s, (bs, 1))
    tk_s = jnp.tile(tk_s, (bs, 1))
    g_fin = g_final[None]

    xp = x_prompt.reshape(n_p, D_MODEL)
    xs = x_sample.reshape(n_s, D_MODEL)
    outs = [[] for _ in range(8)]
    for l in range(depth):
        lambda_init = 0.8 - 0.6 * math.exp(-0.3 * l)
        last = l == depth - 1
        w = _prep_layer(l, w_in, g_attn, g_q_a, w_uq, g_kv_a, w_uk, w_uv, lambda_q1, lambda_k1,
                        lambda_q2, lambda_k2, g_subln, w_o, g_ffn, w_router, b_router,
                        w_gate_up, b_gate_up, w_down, b_down)

        tm_p = _token_tile(sp, 512)
        ckv, kr, dk, dv, qm, dq, kvc, dkb, dvb = _proj_call(xp, tq_p, tk_p, w, tm_p)
        o_p = _prompt_attn_call(qm, dq, kvc, dkb, dvb, w, bp, sp, lambda_init)
        hp, xn2, ids, rk, gt, cnt = _outproj_router_call(o_p, xp, w, tm_p)
        xp = _moe(xn2, ids, rk, gt, cnt, hp, w, g_fin, last, tm_p)
        outs[0].append(ckv.reshape(bp, sp, KV_LORA))
        outs[1].append(kr.reshape(bp, sp, MLA_ROPE))
        outs[2].append(dk.reshape(bp, sp, DIFF_HEADS, 2 * DIFF_QK))
        outs[3].append(dv.reshape(bp, sp, DIFF_HEADS, DIFF_V))

        tm_s = _token_tile(n_s, 512)
        ckv_s, kr_s, dk_s, dv_s, qm_s, dq_s, kvc_s, dkb_s, dvb_s = _proj_call(
            xs, tq_s, tk_s, w, tm_s)
        q_s = qm_s.reshape(MLA_HEADS, bs, ss, KVC).transpose(1, 2, 0, 3).reshape(
            bs, ss * MLA_HEADS, KVC)
        dq_t = jnp.broadcast_to(
            dq_s.astype(F32).reshape(bs, ss, DIFF_HEADS, 1, DIFF_V).transpose(0, 2, 3, 1, 4),
            (bs, DIFF_HEADS, 2, ss, DIFF_V)).reshape(bs, DIFF_HEADS * 2 * ss, DIFF_V)
        padk = lambda a: jnp.pad(a.reshape(bs, ss, a.shape[-1]),
                                 ((0, 0), (0, new_pad - ss), (0, 0)))
        caches = (cache_ckv[l], jnp.swapaxes(cache_krope[l], 1, 2),
                  cache_diff_k[l].reshape(n_pool, page * DIFF_HEADS, DIFF_V),
                  cache_diff_v[l].reshape(n_pool, page * DIFF_HEADS, DIFF_V))
        o_s = _sample_attn_call(page_table, q_s, dq_t, padk(kvc_s), padk(dkb_s), padk(dvb_s),
                                caches, w, lambda_init)
        hs, xn2_s, ids_s, rk_s, gt_s, cnt_s = _outproj_router_call(
            o_s.reshape(n_s, D_MODEL), xs, w, tm_s)
        xs = _moe(xn2_s, ids_s, rk_s, gt_s, cnt_s, hs, w, g_fin, last, tm_s)
        outs[4].append(ckv_s.reshape(bs, ss, KV_LORA))
        outs[5].append(kr_s.reshape(bs, ss, MLA_ROPE))
        outs[6].append(dk_s.reshape(bs, ss, DIFF_HEADS, 2 * DIFF_QK))
        outs[7].append(dv_s.reshape(bs, ss, DIFF_HEADS, DIFF_V))

    return (xp.reshape(bp, sp, D_MODEL), xs.reshape(bs, ss, D_MODEL),
            *[jnp.stack(o) for o in outs])
```
